```python
import math
import jax, jax.numpy as jnp
from jax import lax
import numpy as np

D_MODEL = 1024
BATCH = 8
SEQ = 4096
DEPTH = 2

EPS = 1e-6
A_HEADS = 8
A_DHEAD = 64
A_QK = A_HEADS * 2 * A_DHEAD
A_V = A_HEADS * 2 * A_DHEAD
Q_BLOCK = 128
M_HEADS = 8
M_DQK = 64
M_DV = 128
M_QK = M_HEADS * M_DQK
M_V = M_HEADS * M_DV
CONV_K = 4
CHUNK = 64
F_BIAS_OFFSET = 3.0
D_FF = 2816
FFN_SCALE = 0.5
SPLIT_SIZES = [A_QK, A_QK, A_V, M_QK, M_QK, M_V, M_V, M_HEADS, M_HEADS, 2 * D_MODEL]
SPLIT_IDX = [int(v) for v in np.cumsum(SPLIT_SIZES)[:-1]]
C_IN = int(sum(SPLIT_SIZES))

kernel_name = "hybrid_diffattn_mlstm_macaron"


def rmsnorm(x, w):
    xf = x.astype(jnp.float32)
    y = xf * lax.rsqrt(jnp.mean(xf * xf, axis=-1, keepdims=True) + EPS)
    return (y * w.astype(jnp.float32)).astype(x.dtype)


def swiglu(h, w_gu, w_down):
    g, u = jnp.split(h @ w_gu, 2, axis=-1)
    return (jax.nn.silu(g) * u) @ w_down


def causal_depthwise_conv(x, w, b):
    c = x.shape[-1]
    y = lax.conv_general_dilated(x, w[:, None, :].astype(x.dtype), window_strides=(1,),
                                 padding=[(CONV_K - 1, 0)],
                                 dimension_numbers=('NWC', 'WIO', 'NWC'),
                                 feature_group_count=c)
    return y + b


def diff_attention(q, k, v, lam, lam_init, norm_w):
    b_, s_, _ = q.shape
    nb = s_ // Q_BLOCK
    scale = A_DHEAD ** -0.5
    qb = q.reshape(b_, nb, Q_BLOCK, A_HEADS, 2, A_DHEAD).transpose(1, 0, 3, 4, 2, 5)
    kh = k.reshape(b_, s_, A_HEADS, 2, A_DHEAD).transpose(0, 2, 3, 1, 4)
    vh = v.reshape(b_, s_, A_HEADS, 2 * A_DHEAD).transpose(0, 2, 1, 3)
    k_pos = jnp.arange(s_)
    lam32 = lam.astype(jnp.float32)

    def block(args):
        qblk, idx = args
        sc = jnp.einsum('bhmqd,bhmkd->bhmqk', qblk, kh).astype(jnp.float32) * scale
        q_pos = idx * Q_BLOCK + jnp.arange(Q_BLOCK)
        causal = k_pos[None, :] <= q_pos[:, None]
        sc = jnp.where(causal, sc, -jnp.inf)
        p = jax.nn.softmax(sc, axis=-1)
        a = p[:, :, 0] - lam32 * p[:, :, 1]
        return jnp.einsum('bhqk,bhkv->bhqv', a.astype(vh.dtype), vh)

    o = lax.map(block, (qb, jnp.arange(nb)))
    o = o.transpose(1, 0, 3, 2, 4).reshape(b_, s_, A_HEADS, 2 * A_DHEAD)
    o = rmsnorm(o, norm_w) * (1.0 - lam_init)
    return o.reshape(b_, s_, A_V)


def mlstm_chunkwise(q, k, v, ig, fg):
    b_, s_ = q.shape[:2]
    nc = s_ // CHUNK
    f32 = jnp.float32

    def chunk(t, d):
        return t.astype(f32).reshape(b_, nc, CHUNK, M_HEADS, d).transpose(0, 3, 1, 2, 4)

    qc = chunk(q, M_DQK) * (M_DQK ** -0.5)
    kc = chunk(k, M_DQK)
    vc = chunk(v, M_DV)
    igc = ig.astype(f32).reshape(b_, nc, CHUNK, M_HEADS).transpose(0, 3, 1, 2)
    logf = jax.nn.log_sigmoid(fg.astype(f32)).reshape(b_, nc, CHUNK, M_HEADS).transpose(0, 3, 1, 2)
    bcum = jnp.cumsum(logf, axis=-1)
    b_last = bcum[..., -1]

    a = b_last[..., None] - bcum + igc
    m_loc = jnp.max(a, axis=-1)
    w = jnp.exp(a - m_loc[..., None])
    c_loc = jnp.einsum('bhcl,bhcld,bhclv->bhcdv', w, kc, vc)
    n_loc = jnp.einsum('bhcl,bhcld->bhcd', w, kc)

    def step(carry, xs):
        c_st, n_st, m_st = carry
        bl, ml, cl, nl = xs
        m_new = jnp.maximum(bl + m_st, ml)
        sp = jnp.exp(bl + m_st - m_new)
        sl = jnp.exp(ml - m_new)
        c_new = sp[..., None, None] * c_st + sl[..., None, None] * cl
        n_new = sp[..., None] * n_st + sl[..., None] * nl
        return (c_new, n_new, m_new), (c_st, n_st, m_st)

    init = (jnp.zeros((b_, M_HEADS, M_DQK, M_DV), f32),
            jnp.zeros((b_, M_HEADS, M_DQK), f32),
            jnp.zeros((b_, M_HEADS), f32))
    xs = (jnp.moveaxis(b_last, 2, 0), jnp.moveaxis(m_loc, 2, 0),
          jnp.moveaxis(c_loc, 2, 0), jnp.moveaxis(n_loc, 2, 0))
    _, (c_prev, n_prev, m_prev) = lax.scan(step, init, xs)
    c_prev = jnp.moveaxis(c_prev, 0, 2)
    n_prev = jnp.moveaxis(n_prev, 0, 2)
    m_prev = jnp.moveaxis(m_prev, 0, 2)

    tril = jnp.tril(jnp.ones((CHUNK, CHUNK), dtype=bool))
    dmat = bcum[..., :, None] - bcum[..., None, :] + igc[..., None, :]
    dmat = jnp.where(tril, dmat, -jnp.inf)
    m_inter = bcum + m_prev[..., None]
    m_t = jnp.maximum(m_inter, jnp.max(dmat, axis=-1))
    sc_inter = jnp.exp(m_inter - m_t)
    p = jnp.exp(dmat - m_t[..., None]) * jnp.einsum('bhcld,bhcsd->bhcls', qc, kc)
    num = sc_inter[..., None] * jnp.einsum('bhcld,bhcdv->bhclv', qc, c_prev) + \
        jnp.einsum('bhcls,bhcsv->bhclv', p, vc)
    den = sc_inter * jnp.einsum('bhcld,bhcd->bhcl', qc, n_prev) + jnp.sum(p, axis=-1)
    h = num / jnp.maximum(jnp.abs(den), jnp.exp(-m_t))[..., None]
    h = h.transpose(0, 2, 3, 1, 4).reshape(b_, s_, M_HEADS, M_DV)
    return h.astype(q.dtype)


def setup_inputs(seed: int = 0) -> dict:
    key = jax.random.key(seed)
    ks = iter(jax.random.split(key, 32))
    L = DEPTH

    def nrm(shape, scale):
        return scale * jax.random.normal(next(ks), shape, jnp.float32)

    def gain(shape):
        return 1.0 + nrm(shape, 0.05)

    return {
        "x": nrm((BATCH, SEQ, D_MODEL), 1.0),
        "ffn1_norm_pre": gain((L, D_MODEL)),
        "ffn1_w_gu": nrm((L, D_MODEL, 2 * D_FF), D_MODEL ** -0.5),
        "ffn1_w_down": nrm((L, D_FF, D_MODEL), D_FF ** -0.5),
        "ffn1_norm_post": gain((L, D_MODEL)),
        "mix_norm_pre": gain((L, D_MODEL)),
        "w_in": nrm((L, D_MODEL, C_IN), D_MODEL ** -0.5),
        "attn_lam_q1": nrm((L, A_DHEAD), 0.1),
        "attn_lam_k1": nrm((L, A_DHEAD), 0.1),
        "attn_lam_q2": nrm((L, A_DHEAD), 0.1),
        "attn_lam_k2": nrm((L, A_DHEAD), 0.1),
        "attn_norm_w": gain((L, 2 * A_DHEAD)),
        "conv_w": nrm((L, CONV_K, 2 * M_QK), CONV_K ** -0.5),
        "conv_b": nrm((L, 2 * M_QK), 0.02),
        "igate_b": nrm((L, M_HEADS), 0.1),
        "fgate_b": F_BIAS_OFFSET + nrm((L, M_HEADS), 0.5),
        "mlstm_norm_w": gain((L, M_HEADS, M_DV)),
        "w_proj_a": nrm((L, A_V, D_MODEL), A_V ** -0.5),
        "w_proj_m": nrm((L, M_V, D_MODEL), M_V ** -0.5),
        "gate_b": nrm((L, 2 * D_MODEL), 0.01),
        "w_out": nrm((L, D_MODEL, D_MODEL), D_MODEL ** -0.5),
        "mix_norm_post": gain((L, D_MODEL)),
        "ffn2_norm_pre": gain((L, D_MODEL)),
        "ffn2_w_gu": nrm((L, D_MODEL, 2 * D_FF), D_MODEL ** -0.5),
        "ffn2_w_down": nrm((L, D_FF, D_MODEL), D_FF ** -0.5),
        "ffn2_norm_post": gain((L, D_MODEL)),
    }


def reference(x, ffn1_norm_pre, ffn1_w_gu, ffn1_w_down, ffn1_norm_post, mix_norm_pre, w_in,
              attn_lam_q1, attn_lam_k1, attn_lam_q2, attn_lam_k2, attn_norm_w, conv_w, conv_b,
              igate_b, fgate_b, mlstm_norm_w, w_proj_a, w_proj_m, gate_b, w_out, mix_norm_post,
              ffn2_norm_pre, ffn2_w_gu, ffn2_w_down, ffn2_norm_post):
    b_, s_, _ = x.shape
    for l in range(DEPTH):
        h = rmsnorm(x, ffn1_norm_pre[l])
        x = x + FFN_SCALE * rmsnorm(swiglu(h, ffn1_w_gu[l], ffn1_w_down[l]), ffn1_norm_post[l])

        h = rmsnorm(x, mix_norm_pre[l])
        z = h @ w_in[l]
        aq, ak, av, mq, mk, mv, mo, mi, mf, gl = jnp.split(z, SPLIT_IDX, axis=-1)

        lam_init = 0.8 - 0.6 * math.exp(-0.3 * l)
        lam = (jnp.exp(jnp.sum(attn_lam_q1[l] * attn_lam_k1[l]))
               - jnp.exp(jnp.sum(attn_lam_q2[l] * attn_lam_k2[l])) + lam_init)
        ya = diff_attention(aq, ak, av, lam, lam_init, attn_norm_w[l])

        mqk = jax.nn.silu(causal_depthwise_conv(jnp.concatenate([mq, mk], axis=-1), conv_w[l], conv_b[l]))
        mq_c, mk_c = jnp.split(mqk, 2, axis=-1)
        hm = mlstm_chunkwise(mq_c.reshape(b_, s_, M_HEADS, M_DQK),
                             mk_c.reshape(b_, s_, M_HEADS, M_DQK),
                             mv.reshape(b_, s_, M_HEADS, M_DV),
                             mi + igate_b[l], mf + fgate_b[l])
        ym = (jax.nn.sigmoid(mo) * rmsnorm(hm, mlstm_norm_w[l]).reshape(b_, s_, M_V))

        g_a, g_m = jnp.split(jax.nn.sigmoid(gl + gate_b[l]), 2, axis=-1)
        merged = g_a * (ya @ w_proj_a[l]) + g_m * (ym @ w_proj_m[l])
        x = x + rmsnorm(merged @ w_out[l], mix_norm_post[l])

        h = rmsnorm(x, ffn2_norm_pre[l])
        x = x + FFN_SCALE * rmsnorm(swiglu(h, ffn2_w_gu[l], ffn2_w_down[l]), ffn2_norm_post[l])
    return x
```

```python
import functools
import math

import jax
import jax.numpy as jnp
from jax import lax
from jax.experimental import pallas as pl
from jax.experimental.pallas import tpu as pltpu

F32 = jnp.float32
BF16 = jnp.bfloat16

EPS = 1e-6
DEPTH = 2
A_HEADS = 8
A_DHEAD = 64
M_HEADS = 8
M_DQK = 64
M_DV = 128
CONV_K = 4
D_FF = 2816
FFN_SCALE = 0.5

LANES = 128
VMEM_LIMIT_BYTES = 56 * 1024 * 1024

ROW_TILE = 512
FF_CHUNK = 256
PROJ_CHUNK = 512
ATT_TQ = 256
ATT_TK = 256
M_CHUNK = 256
CONV_HALO = 8

_NT = (((1,), (1,)), ((), ()))


def _params(n_axes):
    return pltpu.CompilerParams(dimension_semantics=("arbitrary",) * n_axes,
                                vmem_limit_bytes=VMEM_LIMIT_BYTES)


def _resident(shape):
    nd = len(shape)
    return pl.BlockSpec(shape, lambda *_: (0,) * nd, pipeline_mode=pl.Buffered(1))


def _rms(x):
    return x * lax.rsqrt(jnp.mean(x * x, axis=-1, keepdims=True) + EPS)


def _dot(a, b):
    return jnp.dot(a, b, preferred_element_type=F32)


def _dot_nt(a, b):
    return lax.dot_general(a, b, _NT, preferred_element_type=F32)


def _ffn_kernel(x_ref, npre_ref, wgu_ref, wd_ref, npost_ref, o_ref, h_ref, acc_ref, *, n_chunks):
    x = x_ref[...]
    h_ref[...] = (_rms(x) * npre_ref[...]).astype(BF16)
    for j in range(n_chunks):
        h = h_ref[...]
        g = _dot(h, wgu_ref[j])
        u = _dot(h, wgu_ref[n_chunks + j])
        a = (g * jax.nn.sigmoid(g) * u).astype(BF16)
        d = _dot(a, wd_ref[j])
        if j == 0:
            acc_ref[...] = d
        else:
            acc_ref[...] += d
    o_ref[...] = x + FFN_SCALE * (_rms(acc_ref[...]) * npost_ref[...])


def _ffn(x, npre, wgu, wd, npost):
    n, d = x.shape
    n_chunks = wd.shape[0]
    tm = min(ROW_TILE, n)
    return pl.pallas_call(
        functools.partial(_ffn_kernel, n_chunks=n_chunks),
        grid=(n // tm,),
        in_specs=[
            pl.BlockSpec((tm, d), lambda i: (i, 0)),
            _resident(npre.shape),
            _resident(wgu.shape),
            _resident(wd.shape),
            _resident(npost.shape),
        ],
        out_specs=pl.BlockSpec((tm, d), lambda i: (i, 0)),
        out_shape=jax.ShapeDtypeStruct((n, d), F32),
        scratch_shapes=[pltpu.VMEM((tm, d), BF16), pltpu.VMEM((tm, d), F32)],
        compiler_params=_params(1),
        name="ffn",
    )(x, npre, wgu, wd, npost)


def _mixin_kernel(x_ref, nw_ref, w_ref, wg_ref, z_ref, g_ref, *, n_chunks, cw):
    h = (_rms(x_ref[...]) * nw_ref[...]).astype(BF16)
    for j in range(n_chunks):
        z_ref[:, j * cw:(j + 1) * cw] = _dot(h, w_ref[j]).astype(BF16)
    g_ref[...] = _dot(h, wg_ref[...])


def _mixin(x, nw, w, wg):
    n, d = x.shape
    n_chunks, _, cw = w.shape
    tm = min(ROW_TILE, n)
    return pl.pallas_call(
        functools.partial(_mixin_kernel, n_chunks=n_chunks, cw=cw),
        grid=(n // tm,),
        in_specs=[
            pl.BlockSpec((tm, d), lambda i: (i, 0)),
            _resident(nw.shape),
            _resident(w.shape),
            _resident(wg.shape),
        ],
        out_specs=[
            pl.BlockSpec((tm, n_chunks * cw), lambda i: (i, 0)),
            pl.BlockSpec((tm, LANES), lambda i: (i, 0)),
        ],
        out_shape=[
            jax.ShapeDtypeStruct((n, n_chunks * cw), BF16),
            jax.ShapeDtypeStruct((n, LANES), F32),
        ],
        compiler_params=_params(1),
        name="mixin",
    )(x, nw, w, wg)


def _attn_kernel(q_ref, k_ref, v_ref, lam_ref, nw_ref, o_ref, vext_ref, m_ref, acc_ref,
                 *, tq, tk, lam_init):
    s_len = q_ref.shape[0]
    dv = v_ref.shape[1]
    lane = lax.broadcasted_iota(jnp.int32, (1, LANES), 1)
    first_map = lane < A_DHEAD

    vext_ref[:, :dv] = v_ref[...]
    vext_ref[:, dv:] = jnp.broadcast_to((lane == 0).astype(BF16), (s_len, LANES))

    lp = lam_ref[...]
    lam = (jnp.exp(jnp.sum(lp[0:1] * lp[1:2], axis=-1, keepdims=True))
           - jnp.exp(jnp.sum(lp[2:3] * lp[3:4], axis=-1, keepdims=True)) + lam_init)

    def step(qq, kj, masked, q_pos0):
        k = k_ref[pl.ds(kj * tk, tk), :]
        s = _dot_nt(qq, k)
        if masked:
            q_pos = q_pos0 + lax.broadcasted_iota(jnp.int32, (tq, tk), 0)
            k_pos = kj * tk + lax.broadcasted_iota(jnp.int32, (tq, tk), 1)
            ok = k_pos <= q_pos
            ok = jnp.concatenate([ok, ok], axis=0)
            s = jnp.where(ok, s, -jnp.inf)
        m_old = m_ref[...]
        m_new = jnp.maximum(m_old, jnp.max(s, axis=-1, keepdims=True))
        alpha = jnp.exp(m_old - m_new)
        p = jnp.exp(s - m_new).astype(BF16)
        pv = _dot(p, vext_ref[pl.ds(kj * tk, tk), :])
        acc_ref[...] = alpha * acc_ref[...] + pv
        m_ref[...] = m_new

    def q_block(qi, carry):
        q = q_ref[pl.ds(qi * tq, tq), :] * (A_DHEAD ** -0.5)
        zero = jnp.zeros_like(q)
        qq = jnp.concatenate([jnp.where(first_map, q, zero), jnp.where(first_map, zero, q)], axis=0)
        m_ref[...] = jnp.full(m_ref.shape, -jnp.inf, F32)
        acc_ref[...] = jnp.zeros(acc_ref.shape, F32)
        q_pos0 = qi * tq
        n_full = (qi * tq) // tk

        def full_body(kj, c):
            step(qq, kj, False, q_pos0)
            return c

        lax.fori_loop(0, n_full, full_body, 0)
        for d in range(tq // tk if tq >= tk else 1):
            step(qq, n_full + d, True, q_pos0)

        acc = acc_ref[...]
        o1 = acc[:tq, :dv] / acc[:tq, dv:dv + 1]
        o2 = acc[tq:, :dv] / acc[tq:, dv:dv + 1]
        o = o1 - lam * o2
        o = _rms(o) * nw_ref[...] * (1.0 - lam_init)
        o_ref[pl.ds(qi * tq, tq), :] = o.astype(o_ref.dtype)
        return carry

    lax.fori_loop(0, s_len // tq, q_block, 0)


def _attention(z3, lam_p, nw, lam_init):
    b, s, _ = z3.shape
    dq = 2 * A_DHEAD
    tq = min(ATT_TQ, s)
    tk = min(ATT_TK, s)
    assert tq == tk
    return pl.pallas_call(
        functools.partial(_attn_kernel, tq=tq, tk=tk, lam_init=lam_init),
        grid=(b, A_HEADS),
        in_specs=[
            pl.BlockSpec((None, s, dq), lambda i, h: (i, 0, h)),
            pl.BlockSpec((None, s, dq), lambda i, h: (i, 0, A_HEADS + h)),
            pl.BlockSpec((None, s, dq), lambda i, h: (i, 0, 2 * A_HEADS + h)),
            _resident(lam_p.shape),
            _resident(nw.shape),
        ],
        out_specs=pl.BlockSpec((None, s, dq), lambda i, h: (i, 0, h)),
        out_shape=jax.ShapeDtypeStruct((b, s, A_HEADS * dq), BF16),
        scratch_shapes=[
            pltpu.VMEM((s, dq + LANES), BF16),
            pltpu.VMEM((2 * tq, 1), F32),
            pltpu.VMEM((2 * tq, dq + LANES), F32),
        ],
        compiler_params=_params(2),
        name="diff_attn",
    )(z3, z3, z3, lam_p, nw)


def _mlstm_kernel(qk_ref, v_ref, og_ref, gn_ref, gt_ref, cw_ref, cb_ref, bn_ref, bt_ref, nw_ref,
                  o_ref, xbuf_ref, c_ref, m_ref, *, lc):
    nqk = M_HEADS * M_DQK
    c_idx = pl.program_id(1)

    @pl.when(c_idx == 0)
    def _():
        xbuf_ref[0:CONV_HALO, :] = jnp.zeros((CONV_HALO, xbuf_ref.shape[1]), F32)
        c_ref[...] = jnp.zeros(c_ref.shape, F32)
        m_ref[...] = jnp.zeros(m_ref.shape, F32)

    xbuf_ref[CONV_HALO:, :] = qk_ref[...].astype(F32)
    y = cb_ref[...]
    for i in range(CONV_K):
        off = CONV_HALO - (CONV_K - 1) + i
        y = y + cw_ref[i:i + 1, :] * xbuf_ref[off:off + lc, :]
    xbuf_ref[0:CONV_HALO, :] = xbuf_ref[lc:lc + CONV_HALO, :]
    a = y * jax.nn.sigmoid(y)
    q_all = a[:, :nqk] * (M_DQK ** -0.5)
    k_all = a[:, nqk:]

    gn = gn_ref[...] + bn_ref[...]
    logf_n = jax.nn.log_sigmoid(gn)
    r_i = lax.broadcasted_iota(jnp.int32, (lc, lc), 0)
    c_i = lax.broadcasted_iota(jnp.int32, (lc, lc), 1)
    causal = c_i <= r_i
    tri = causal.astype(F32)
    bcum_n = jnp.dot(tri, logf_n, precision=lax.Precision.HIGHEST, preferred_element_type=F32)
    gt = gt_ref[...] + bt_ref[...]
    ig_t = gt[0:M_HEADS]
    logf_t = jax.nn.log_sigmoid(gt[M_HEADS:2 * M_HEADS])
    bcum_t = lax.dot_general(logf_t, tri, _NT, precision=lax.Precision.HIGHEST,
                             preferred_element_type=F32)
    b_last = bcum_t[:, lc - 1:lc]

    lane = lax.broadcasted_iota(jnp.int32, (1, LANES), 1)
    sub = lax.broadcasted_iota(jnp.int32, (LANES, 1), 0)
    ones_col = jnp.broadcast_to((lane == 0).astype(BF16), (lc, LANES))

    for pj in range(M_HEADS // 2):
        qp = q_all[:, pj * LANES:(pj + 1) * LANES]
        kp = k_all[:, pj * LANES:(pj + 1) * LANES]
        kb = kp.astype(BF16)
        kpt = kp.T
        c_old = c_ref[pj]
        c_bf = c_old.astype(BF16)
        row_scale = jnp.zeros((LANES, 1), F32)
        upd = jnp.zeros(c_old.shape, F32)
        for half in range(2):
            h = 2 * pj + half
            in_head_l = (lane >= half * M_DQK) & (lane < (half + 1) * M_DQK)
            in_head_s = (sub >= half * M_DQK) & (sub < (half + 1) * M_DQK)
            qh = jnp.where(in_head_l, qp, 0.0).astype(BF16)
            vext = jnp.concatenate([v_ref[:, h * M_DV:(h + 1) * M_DV], ones_col], axis=1)

            s = _dot_nt(qh, kb)
            bc = bcum_n[:, M_HEADS + h:M_HEADS + h + 1]
            br = bcum_t[h:h + 1, :]
            ir = ig_t[h:h + 1, :]
            dm = jnp.where(causal, bc - br + ir, -jnp.inf)
            m_prev = m_ref[h:h + 1, 0:1]
            m_inter = bc + m_prev
            m_t = jnp.maximum(m_inter, jnp.max(dm, axis=-1, keepdims=True))
            p = (jnp.exp(dm - m_t) * s).astype(BF16)
            sc = jnp.exp(m_inter - m_t)
            tot = sc * _dot(qh, c_bf) + _dot(p, vext)
            num = tot[:, :M_DV]
            den = tot[:, M_DV:M_DV + 1]
            hh = num / jnp.maximum(jnp.abs(den), jnp.exp(-m_t))
            hn = _rms(hh) * nw_ref[h:h + 1, :]
            og = og_ref[:, h * M_DV:(h + 1) * M_DV].astype(F32)
            o_ref[:, h * M_DV:(h + 1) * M_DV] = (jax.nn.sigmoid(og) * hn).astype(o_ref.dtype)

            bl = b_last[h:h + 1, :]
            a_row = bl - br + ir
            m_new = jnp.maximum(bl + m_prev, jnp.max(a_row, axis=-1, keepdims=True))
            w_row = jnp.exp(a_row - m_new)
            sp = jnp.exp(bl + m_prev - m_new)
            kw = jnp.where(in_head_s, kpt * w_row, 0.0).astype(BF16)
            upd = upd + _dot(kw, vext)
            row_scale = jnp.where(in_head_s, sp, row_scale)
            m_ref[h:h + 1, :] = jnp.broadcast_to(m_new, (1, LANES))
        c_ref[pj] = row_scale * c_old + upd


def _mlstm(z3, gates_n, gates_t, cw, cb, bias_n, bias_t, nw):
    b, s, _ = z3.shape
    lc = min(M_CHUNK, s)
    nqk2 = 2 * M_HEADS * M_DQK
    nv = M_HEADS * M_DV
    qk_blk = (3 * A_HEADS * 2 * A_DHEAD) // nqk2
    v_blk = qk_blk + 1
    og_blk = qk_blk + 2
    return pl.pallas_call(
        functools.partial(_mlstm_kernel, lc=lc),
        grid=(b, s // lc),
        in_specs=[
            pl.BlockSpec((None, lc, nqk2), lambda i, c: (i, c, qk_blk)),
            pl.BlockSpec((None, lc, nv), lambda i, c: (i, c, v_blk)),
            pl.BlockSpec((None, lc, nv), lambda i, c: (i, c, og_blk)),
            pl.BlockSpec((None, lc, LANES), lambda i, c: (i, c, 0)),
            pl.BlockSpec((None, 2 * M_HEADS, lc), lambda i, c: (i, 0, c)),
            _resident(cw.shape),
            _resident(cb.shape),
            _resident(bias_n.shape),
            _resident(bias_t.shape),
            _resident(nw.shape),
        ],
        out_specs=pl.BlockSpec((None, lc, nv), lambda i, c: (i, c, 0)),
        out_shape=jax.ShapeDtypeStruct((b, s, nv), BF16),
        scratch_shapes=[
            pltpu.VMEM((lc + CONV_HALO, nqk2), F32),
            pltpu.VMEM((M_HEADS // 2, LANES, M_DV + LANES), F32),
            pltpu.VMEM((M_HEADS, LANES), F32),
        ],
        compiler_params=_params(2),
        name="mlstm",
    )(z3, z3, z3, gates_n, gates_t, cw, cb, bias_n, bias_t, nw)


def _merge_kernel(x_ref, ya_ref, ym_ref, gl_ref, gb_ref, wa_ref, wm_ref, wo_ref, nw_ref, o_ref):
    d = x_ref.shape[1]
    g = jax.nn.sigmoid(gl_ref[...].astype(F32) + gb_ref[...])
    merged = g[:, :d] * _dot(ya_ref[...], wa_ref[...]) + g[:, d:] * _dot(ym_ref[...], wm_ref[...])
    y = _dot(merged.astype(BF16), wo_ref[...])
    o_ref[...] = x_ref[...] + _rms(y) * nw_ref[...]


def _merge(x, ya, ym, z, gb, wa, wm, wo, nw):
    n, d = x.shape
    tm = min(ROW_TILE, n)
    gl_blk = (z.shape[1] - 2 * d) // (2 * d)
    return pl.pallas_call(
        _merge_kernel,
        grid=(n // tm,),
        in_specs=[
            pl.BlockSpec((tm, d), lambda i: (i, 0)),
            pl.BlockSpec((tm, d), lambda i: (i, 0)),
            pl.BlockSpec((tm, d), lambda i: (i, 0)),
            pl.BlockSpec((tm, 2 * d), lambda i: (i, gl_blk)),
            _resident(gb.shape),
            _resident(wa.shape),
            _resident(wm.shape),
            _resident(wo.shape),
            _resident(nw.shape),
        ],
        out_specs=pl.BlockSpec((tm, d), lambda i: (i, 0)),
        out_shape=jax.ShapeDtypeStruct((n, d), F32),
        compiler_params=_params(1),
        name="merge",
    )(x, ya, ym, z, gb, wa, wm, wo, nw)


def _prep_ffn(w_gu, w_down):
    d, two_ff = w_gu.shape
    n_chunks = two_ff // 2 // FF_CHUNK
    wgu = w_gu.astype(BF16).reshape(d, 2 * n_chunks, FF_CHUNK).transpose(1, 0, 2)
    wd = w_down.astype(BF16).reshape(n_chunks, FF_CHUNK, d)
    return wgu, wd


def kernel(x, ffn1_norm_pre, ffn1_w_gu, ffn1_w_down, ffn1_norm_post, mix_norm_pre, w_in,
           attn_lam_q1, attn_lam_k1, attn_lam_q2, attn_lam_k2, attn_norm_w, conv_w, conv_b,
           igate_b, fgate_b, mlstm_norm_w, w_proj_a, w_proj_m, gate_b, w_out, mix_norm_post,
           ffn2_norm_pre, ffn2_w_gu, ffn2_w_down, ffn2_norm_post):
    b, s, d = x.shape
    n = b * s
    a_w = 3 * A_HEADS * 2 * A_DHEAD
    m_w = 2 * M_HEADS * M_DQK + 2 * M_HEADS * M_DV
    g0 = a_w + m_w
    g1 = g0 + 2 * M_HEADS
    row = lambda v: v.reshape(1, -1)

    xf = x.reshape(n, d)
    for l in range(DEPTH):
        wgu, wd = _prep_ffn(ffn1_w_gu[l], ffn1_w_down[l])
        xf = _ffn(xf, row(ffn1_norm_pre[l]), wgu, wd, row(ffn1_norm_post[l]))

        w_main = jnp.concatenate([w_in[l][:, :g0], w_in[l][:, g1:]], axis=1).astype(BF16)
        n_pc = w_main.shape[1] // PROJ_CHUNK
        w_main = w_main.reshape(d, n_pc, PROJ_CHUNK).transpose(1, 0, 2)
        w_gate = jnp.pad(w_in[l][:, g0:g1], ((0, 0), (0, LANES - 2 * M_HEADS))).astype(BF16)
        z, gates = _mixin(xf, row(mix_norm_pre[l]), w_main, w_gate)
        z3 = z.reshape(b, s, -1)

        lam_init = 0.8 - 0.6 * math.exp(-0.3 * l)
        lam_p = jnp.stack([attn_lam_q1[l], attn_lam_k1[l], attn_lam_q2[l], attn_lam_k2[l]])
        ya = _attention(z3, lam_p, row(attn_norm_w[l]), lam_init)

        gates_n = gates.reshape(b, s, LANES)
        gates_t = gates_n[:, :, :2 * M_HEADS].transpose(0, 2, 1)
        gate_bias = jnp.concatenate([igate_b[l], fgate_b[l]])
        bias_n = jnp.pad(gate_bias, (0, LANES - 2 * M_HEADS)).reshape(1, LANES)
        bias_t = gate_bias.reshape(2 * M_HEADS, 1)
        ym = _mlstm(z3, gates_n, gates_t, conv_w[l], row(conv_b[l]), bias_n, bias_t, mlstm_norm_w[l])

        xf = _merge(xf, ya.reshape(n, -1), ym.reshape(n, -1), z, row(gate_b[l]),
                    w_proj_a[l].astype(BF16), w_proj_m[l].astype(BF16), w_out[l].astype(BF16),
                    row(mix_norm_post[l]))

        wgu, wd = _prep_ffn(ffn2_w_gu[l], ffn2_w_down[l])
        xf = _ffn(xf, row(ffn2_norm_pre[l]), wgu, wd, row(ffn2_norm_post[l]))
    return xf.reshape(b, s, d)
```

```python
import functools
import math

import jax
import jax.numpy as jnp
from jax import lax
from jax.experimental import pallas as pl
from jax.experimental.pallas import tpu as pltpu

F32 = jnp.float32
BF16 = jnp.bfloat16

EPS = 1e-6
DEPTH = 2
A_HEADS = 8
A_DHEAD = 64
M_HEADS = 8
M_DQK = 64
M_DV = 128
CONV_K = 4
D_FF = 2816
FFN_SCALE = 0.5

LANES = 128
VMEM_LIMIT_BYTES = 56 * 1024 * 1024

ROW_TILE = 512
FF_CHUNK = 256
PROJ_CHUNK = 512
ATT_TQ = 512
ATT_TK = 512
M_CHUNK = 256
CONV_HALO = 8

_NT = (((1,), (1,)), ((), ()))


def _params(n_axes):
    return pltpu.CompilerParams(dimension_semantics=("arbitrary",) * n_axes,
                                vmem_limit_bytes=VMEM_LIMIT_BYTES)


def _resident(shape):
    nd = len(shape)
    return pl.BlockSpec(shape, lambda *_: (0,) * nd, pipeline_mode=pl.Buffered(1))


def _rms(x):
    return x * lax.rsqrt(jnp.mean(x * x, axis=-1, keepdims=True) + EPS)


def _dot(a, b):
    return jnp.dot(a, b, preferred_element_type=F32)


def _dot_nt(a, b):
    return lax.dot_general(a, b, _NT, preferred_element_type=F32)


def _ffn_kernel(x_ref, npre_ref, wgu_ref, wd_ref, npost_ref, o_ref, h_ref, acc_ref, *, n_chunks):
    x = x_ref[...]
    h_ref[...] = (_rms(x) * npre_ref[...]).astype(BF16)
    for j in range(n_chunks):
        h = h_ref[...]
        g = _dot(h, wgu_ref[j])
        u = _dot(h, wgu_ref[n_chunks + j])
        a = (g * jax.nn.sigmoid(g) * u).astype(BF16)
        d = _dot(a, wd_ref[j])
        if j == 0:
            acc_ref[...] = d
        else:
            acc_ref[...] += d
    o_ref[...] = x + FFN_SCALE * (_rms(acc_ref[...]) * npost_ref[...])


def _ffn(x, npre, wgu, wd, npost):
    n, d = x.shape
    n_chunks = wd.shape[0]
    tm = min(ROW_TILE, n)
    return pl.pallas_call(
        functools.partial(_ffn_kernel, n_chunks=n_chunks),
        grid=(n // tm,),
        in_specs=[
            pl.BlockSpec((tm, d), lambda i: (i, 0)),
            _resident(npre.shape),
            _resident(wgu.shape),
            _resident(wd.shape),
            _resident(npost.shape),
        ],
        out_specs=pl.BlockSpec((tm, d), lambda i: (i, 0)),
        out_shape=jax.ShapeDtypeStruct((n, d), F32),
        scratch_shapes=[pltpu.VMEM((tm, d), BF16), pltpu.VMEM((tm, d), F32)],
        compiler_params=_params(1),
        name="ffn",
    )(x, npre, wgu, wd, npost)


def _mixin_kernel(x_ref, nw_ref, w_ref, wg_ref, z_ref, g_ref, *, n_chunks, cw):
    h = (_rms(x_ref[...]) * nw_ref[...]).astype(BF16)
    for j in range(n_chunks):
        z_ref[:, j * cw:(j + 1) * cw] = _dot(h, w_ref[j]).astype(BF16)
    g_ref[...] = _dot(h, wg_ref[...])


def _mixin(x, nw, w, wg):
    n, d = x.shape
    n_chunks, _, cw = w.shape
    tm = min(ROW_TILE, n)
    return pl.pallas_call(
        functools.partial(_mixin_kernel, n_chunks=n_chunks, cw=cw),
        grid=(n // tm,),
        in_specs=[
            pl.BlockSpec((tm, d), lambda i: (i, 0)),
            _resident(nw.shape),
            _resident(w.shape),
            _resident(wg.shape),
        ],
        out_specs=[
            pl.BlockSpec((tm, n_chunks * cw), lambda i: (i, 0)),
            pl.BlockSpec((tm, LANES), lambda i: (i, 0)),
        ],
        out_shape=[
            jax.ShapeDtypeStruct((n, n_chunks * cw), BF16),
            jax.ShapeDtypeStruct((n, LANES), F32),
        ],
        compiler_params=_params(1),
        name="mixin",
    )(x, nw, w, wg)


def _attn_kernel(q_ref, k_ref, v_ref, lam_ref, nw_ref, o_ref,
                 qq_ref, vt_ref, bias_ref, s_ref, p_ref, alpha_ref, l_ref, m_ref, lrun_ref, acc_ref,
                 *, tq, tk, lam_init):
    s_len, dv = v_ref.shape
    nq = s_len // tq
    n_blocks = nq * (nq + 1) // 2
    lane = lax.broadcasted_iota(jnp.int32, (1, LANES), 1)
    first_map = lane < A_DHEAD
    q_scale = (A_DHEAD ** -0.5) * math.log2(math.e)

    def vt_body(j, c):
        vt_ref[j] = v_ref[pl.ds(j * tk, tk), :].astype(F32).T.astype(BF16)
        return c

    lax.fori_loop(0, s_len // tk, vt_body, 0)

    def qq_body(j, c):
        q = (q_ref[pl.ds(j * tq, tq), :].astype(F32) * q_scale).astype(BF16)
        zero = jnp.zeros_like(q)
        qq_ref[j, 0:tq, :] = jnp.where(first_map, q, zero)
        qq_ref[j, tq:2 * tq, :] = jnp.where(first_map, zero, q)
        return c

    lax.fori_loop(0, nq, qq_body, 0)

    key = lax.broadcasted_iota(jnp.int32, (tk, 2 * tq), 0)
    col = lax.broadcasted_iota(jnp.int32, (tk, 2 * tq), 1)
    qry = jnp.where(col >= tq, col - tq, col)
    bias_ref[...] = jnp.where(key <= qry, 0.0, -jnp.inf).astype(F32)

    s_ref[...] = jnp.zeros(s_ref.shape, F32)
    p_ref[...] = jnp.zeros(p_ref.shape, BF16)
    alpha_ref[...] = jnp.zeros(alpha_ref.shape, F32)
    l_ref[...] = jnp.zeros(l_ref.shape, F32)
    m_ref[...] = jnp.zeros(m_ref.shape, F32)
    lrun_ref[...] = jnp.zeros(lrun_ref.shape, F32)
    acc_ref[...] = jnp.zeros(acc_ref.shape, F32)

    lp = lam_ref[...]
    lam = (jnp.exp(jnp.sum(lp[0:1] * lp[1:2], axis=-1, keepdims=True))
           - jnp.exp(jnp.sum(lp[2:3] * lp[3:4], axis=-1, keepdims=True)) + lam_init)

    def stages(i, carry, diagonal):
        qa, ka, qb, kb, qc, kc = carry
        slot = lax.rem(i, 2)
        other = 1 - slot

        pv = _dot(vt_ref[kc], p_ref[slot])
        acc_ref[...] = alpha_ref[slot] * acc_ref[...] + pv

        st = s_ref[other]
        if diagonal:
            st = st + bias_ref[...]
        first = kb == 0
        m_old = jnp.where(first, -jnp.inf, m_ref[...])
        l_old = jnp.where(first, 0.0, lrun_ref[...])
        m_new = jnp.maximum(m_old, jnp.max(st, axis=0, keepdims=True))
        alpha = jnp.exp2(m_old - m_new)
        p = jnp.exp2(st - m_new)
        l_new = alpha * l_old + jnp.sum(p, axis=0, keepdims=True)
        p_ref[other] = p.astype(BF16)
        alpha_ref[other] = alpha
        l_ref[other] = l_new
        m_ref[...] = m_new
        lrun_ref[...] = l_new

        s_ref[slot] = _dot_nt(k_ref[pl.ds(ka * tk, tk), :], qq_ref[qa])

        within = ka < qa
        more = qa + 1 < nq
        qa_n = jnp.where(within, qa, jnp.where(more, qa + 1, qa))
        ka_n = jnp.where(within, ka + 1, jnp.where(more, 0, ka))
        return qa_n, ka_n, qa, ka, qb, kb

    def body(i, carry):
        _, _, qb, kb, qc, kc = carry
        new_carry = lax.cond(kb == qb,
                             lambda: stages(i, carry, True),
                             lambda: stages(i, carry, False))

        @pl.when((kc == qc) & (i >= 2))
        def _():
            acc = acc_ref[...]
            l_fin = l_ref[lax.rem(i, 2)]
            o_t = acc[:, :tq] / l_fin[:, :tq] - lam * (acc[:, tq:] / l_fin[:, tq:])
            o = _rms(o_t.T) * nw_ref[...] * (1.0 - lam_init)
            o_ref[pl.ds(qc * tq, tq), :] = o.astype(o_ref.dtype)

        return new_carry

    zero_i = jnp.int32(0)
    lax.fori_loop(0, n_blocks + 2, body, (zero_i,) * 6)


def _attention(z3, lam_p, nw, lam_init):
    b, s, _ = z3.shape
    dq = 2 * A_DHEAD
    tq = min(ATT_TQ, s)
    tk = min(ATT_TK, s)
    assert tq == tk
    return pl.pallas_call(
        functools.partial(_attn_kernel, tq=tq, tk=tk, lam_init=lam_init),
        grid=(b, A_HEADS),
        in_specs=[
            pl.BlockSpec((None, s, dq), lambda i, h: (i, 0, h)),
            pl.BlockSpec((None, s, dq), lambda i, h: (i, 0, A_HEADS + h)),
            pl.BlockSpec((None, s, dq), lambda i, h: (i, 0, 2 * A_HEADS + h)),
            _resident(lam_p.shape),
            _resident(nw.shape),
        ],
        out_specs=pl.BlockSpec((None, s, dq), lambda i, h: (i, 0, h)),
        out_shape=jax.ShapeDtypeStruct((b, s, A_HEADS * dq), BF16),
        scratch_shapes=[
            pltpu.VMEM((s // tq, 2 * tq, dq), BF16),
            pltpu.VMEM((s // tk, dq, tk), BF16),
            pltpu.VMEM((tk, 2 * tq), F32),
            pltpu.VMEM((2, tk, 2 * tq), F32),
            pltpu.VMEM((2, tk, 2 * tq), BF16),
            pltpu.VMEM((2, 1, 2 * tq), F32),
            pltpu.VMEM((2, 1, 2 * tq), F32),
            pltpu.VMEM((1, 2 * tq), F32),
            pltpu.VMEM((1, 2 * tq), F32),
            pltpu.VMEM((dq, 2 * tq), F32),
        ],
        compiler_params=_params(2),
        name="diff_attn",
    )(z3, z3, z3, lam_p, nw)


def _mlstm_kernel(qk_ref, v_ref, og_ref, gn_ref, gt_ref, cw_ref, cb_ref, bn_ref, bt_ref, nw_ref,
                  o_ref, xbuf_ref, c_ref, m_ref, *, lc):
    nqk = M_HEADS * M_DQK
    c_idx = pl.program_id(1)

    @pl.when(c_idx == 0)
    def _():
        xbuf_ref[0:CONV_HALO, :] = jnp.zeros((CONV_HALO, xbuf_ref.shape[1]), F32)
        c_ref[...] = jnp.zeros(c_ref.shape, F32)
        m_ref[...] = jnp.zeros(m_ref.shape, F32)

    xbuf_ref[CONV_HALO:, :] = qk_ref[...].astype(F32)
    y = cb_ref[...]
    for i in range(CONV_K):
        off = CONV_HALO - (CONV_K - 1) + i
        y = y + cw_ref[i:i + 1, :] * xbuf_ref[off:off + lc, :]
    xbuf_ref[0:CONV_HALO, :] = xbuf_ref[lc:lc + CONV_HALO, :]
    a = y * jax.nn.sigmoid(y)
    q_all = a[:, :nqk] * (M_DQK ** -0.5)
    k_all = a[:, nqk:]

    gn = gn_ref[...] + bn_ref[...]
    logf_n = jax.nn.log_sigmoid(gn)
    r_i = lax.broadcasted_iota(jnp.int32, (lc, lc), 0)
    c_i = lax.broadcasted_iota(jnp.int32, (lc, lc), 1)
    causal = c_i <= r_i
    tri = causal.astype(F32)
    bcum_n = jnp.dot(tri, logf_n, precision=lax.Precision.HIGHEST, preferred_element_type=F32)
    gt = gt_ref[...] + bt_ref[...]
    ig_t = gt[0:M_HEADS]
    logf_t = jax.nn.log_sigmoid(gt[M_HEADS:2 * M_HEADS])
    bcum_t = lax.dot_general(logf_t, tri, _NT, precision=lax.Precision.HIGHEST,
                             preferred_element_type=F32)
    b_last = bcum_t[:, lc - 1:lc]

    lane = lax.broadcasted_iota(jnp.int32, (1, LANES), 1)
    sub = lax.broadcasted_iota(jnp.int32, (LANES, 1), 0)
    ones_col = jnp.broadcast_to((lane == 0).astype(BF16), (lc, LANES))

    for pj in range(M_HEADS // 2):
        qp = q_all[:, pj * LANES:(pj + 1) * LANES]
        kp = k_all[:, pj * LANES:(pj + 1) * LANES]
        kb = kp.astype(BF16)
        kpt = kp.T
        c_old = c_ref[pj]
        c_bf = c_old.astype(BF16)
        row_scale = jnp.zeros((LANES, 1), F32)
        upd = jnp.zeros(c_old.shape, F32)
        for half in range(2):
            h = 2 * pj + half
            in_head_l = (lane >= half * M_DQK) & (lane < (half + 1) * M_DQK)
            in_head_s = (sub >= half * M_DQK) & (sub < (half + 1) * M_DQK)
            qh = jnp.where(in_head_l, qp, 0.0).astype(BF16)
            vext = jnp.concatenate([v_ref[:, h * M_DV:(h + 1) * M_DV], ones_col], axis=1)

            s = _dot_nt(qh, kb)
            bc = bcum_n[:, M_HEADS + h:M_HEADS + h + 1]
            br = bcum_t[h:h + 1, :]
            ir = ig_t[h:h + 1, :]
            dm = jnp.where(causal, bc - br + ir, -jnp.inf)
            m_prev = m_ref[h:h + 1, 0:1]
            m_inter = bc + m_prev
            m_t = jnp.maximum(m_inter, jnp.max(dm, axis=-1, keepdims=True))
            p = (jnp.exp(dm - m_t) * s).astype(BF16)
            sc = jnp.exp(m_inter - m_t)
            tot = sc * _dot(qh, c_bf) + _dot(p, vext)
            num = tot[:, :M_DV]
            den = tot[:, M_DV:M_DV + 1]
            hh = num / jnp.maximum(jnp.abs(den), jnp.exp(-m_t))
            hn = _rms(hh) * nw_ref[h:h + 1, :]
            og = og_ref[:, h * M_DV:(h + 1) * M_DV].astype(F32)
            o_ref[:, h * M_DV:(h + 1) * M_DV] = (jax.nn.sigmoid(og) * hn).astype(o_ref.dtype)

            bl = b_last[h:h + 1, :]
            a_row = bl - br + ir
            m_new = jnp.maximum(bl + m_prev, jnp.max(a_row, axis=-1, keepdims=True))
            w_row = jnp.exp(a_row - m_new)
            sp = jnp.exp(bl + m_prev - m_new)
            kw = jnp.where(in_head_s, kpt * w_row, 0.0).astype(BF16)
            upd = upd + _dot(kw, vext)
            row_scale = jnp.where(in_head_s, sp, row_scale)
            m_ref[h:h + 1, :] = jnp.broadcast_to(m_new, (1, LANES))
        c_ref[pj] = row_scale * c_old + upd


def _mlstm(z3, gates_n, gates_t, cw, cb, bias_n, bias_t, nw):
    b, s, _ = z3.shape
    lc = min(M_CHUNK, s)
    nqk2 = 2 * M_HEADS * M_DQK
    nv = M_HEADS * M_DV
    qk_blk = (3 * A_HEADS * 2 * A_DHEAD) // nqk2
    v_blk = qk_blk + 1
    og_blk = qk_blk + 2
    return pl.pallas_call(
        functools.partial(_mlstm_kernel, lc=lc),
        grid=(b, s // lc),
        in_specs=[
            pl.BlockSpec((None, lc, nqk2), lambda i, c: (i, c, qk_blk)),
            pl.BlockSpec((None, lc, nv), lambda i, c: (i, c, v_blk)),
            pl.BlockSpec((None, lc, nv), lambda i, c: (i, c, og_blk)),
            pl.BlockSpec((None, lc, LANES), lambda i, c: (i, c, 0)),
            pl.BlockSpec((None, 2 * M_HEADS, lc), lambda i, c: (i, 0, c)),
            _resident(cw.shape),
            _resident(cb.shape),
            _resident(bias_n.shape),
            _resident(bias_t.shape),
            _resident(nw.shape),
        ],
        out_specs=pl.BlockSpec((None, lc, nv), lambda i, c: (i, c, 0)),
        out_shape=jax.ShapeDtypeStruct((b, s, nv), BF16),
        scratch_shapes=[
            pltpu.VMEM((lc + CONV_HALO, nqk2), F32),
            pltpu.VMEM((M_HEADS // 2, LANES, M_DV + LANES), F32),
            pltpu.VMEM((M_HEADS, LANES), F32),
        ],
        compiler_params=_params(2),
        name="mlstm",
    )(z3, z3, z3, gates_n, gates_t, cw, cb, bias_n, bias_t, nw)


def _merge_kernel(x_ref, ya_ref, ym_ref, gl_ref, gb_ref, wa_ref, wm_ref, wo_ref, nw_ref, o_ref):
    d = x_ref.shape[1]
    g = jax.nn.sigmoid(gl_ref[...].astype(F32) + gb_ref[...])
    merged = g[:, :d] * _dot(ya_ref[...], wa_ref[...]) + g[:, d:] * _dot(ym_ref[...], wm_ref[...])
    y = _dot(merged.astype(BF16), wo_ref[...])
    o_ref[...] = x_ref[...] + _rms(y) * nw_ref[...]


def _merge(x, ya, ym, z, gb, wa, wm, wo, nw):
    n, d = x.shape
    tm = min(ROW_TILE, n)
    gl_blk = (z.shape[1] - 2 * d) // (2 * d)
    return pl.pallas_call(
        _merge_kernel,
        grid=(n // tm,),
        in_specs=[
            pl.BlockSpec((tm, d), lambda i: (i, 0)),
            pl.BlockSpec((tm, d), lambda i: (i, 0)),
            pl.BlockSpec((tm, d), lambda i: (i, 0)),
            pl.BlockSpec((tm, 2 * d), lambda i: (i, gl_blk)),
            _resident(gb.shape),
            _resident(wa.shape),
            _resident(wm.shape),
            _resident(wo.shape),
            _resident(nw.shape),
        ],
        out_specs=pl.BlockSpec((tm, d), lambda i: (i, 0)),
        out_shape=jax.ShapeDtypeStruct((n, d), F32),
        compiler_params=_params(1),
        name="merge",
    )(x, ya, ym, z, gb, wa, wm, wo, nw)


def _prep_ffn(w_gu, w_down):
    d, two_ff = w_gu.shape
    n_chunks = two_ff // 2 // FF_CHUNK
    wgu = w_gu.astype(BF16).reshape(d, 2 * n_chunks, FF_CHUNK).transpose(1, 0, 2)
    wd = w_down.astype(BF16).reshape(n_chunks, FF_CHUNK, d)
    return wgu, wd


def kernel(x, ffn1_norm_pre, ffn1_w_gu, ffn1_w_down, ffn1_norm_post, mix_norm_pre, w_in,
           attn_lam_q1, attn_lam_k1, attn_lam_q2, attn_lam_k2, attn_norm_w, conv_w, conv_b,
           igate_b, fgate_b, mlstm_norm_w, w_proj_a, w_proj_m, gate_b, w_out, mix_norm_post,
           ffn2_norm_pre, ffn2_w_gu, ffn2_w_down, ffn2_norm_post):
    b, s, d = x.shape
    n = b * s
    a_w = 3 * A_HEADS * 2 * A_DHEAD
    m_w = 2 * M_HEADS * M_DQK + 2 * M_HEADS * M_DV
    g0 = a_w + m_w
    g1 = g0 + 2 * M_HEADS
    row = lambda v: v.reshape(1, -1)

    xf = x.reshape(n, d)
    for l in range(DEPTH):
        wgu, wd = _prep_ffn(ffn1_w_gu[l], ffn1_w_down[l])
        xf = _ffn(xf, row(ffn1_norm_pre[l]), wgu, wd, row(ffn1_norm_post[l]))

        w_main = jnp.concatenate([w_in[l][:, :g0], w_in[l][:, g1:]], axis=1).astype(BF16)
        n_pc = w_main.shape[1] // PROJ_CHUNK
        w_main = w_main.reshape(d, n_pc, PROJ_CHUNK).transpose(1, 0, 2)
        w_gate = jnp.pad(w_in[l][:, g0:g1], ((0, 0), (0, LANES - 2 * M_HEADS))).astype(BF16)
        z, gates = _mixin(xf, row(mix_norm_pre[l]), w_main, w_gate)
        z3 = z.reshape(b, s, -1)

        lam_init = 0.8 - 0.6 * math.exp(-0.3 * l)
        lam_p = jnp.stack([attn_lam_q1[l], attn_lam_k1[l], attn_lam_q2[l], attn_lam_k2[l]])
        ya = _attention(z3, lam_p, row(attn_norm_w[l]), lam_init)

        gates_n = gates.reshape(b, s, LANES)
        gates_t = gates_n[:, :, :2 * M_HEADS].transpose(0, 2, 1)
        gate_bias = jnp.concatenate([igate_b[l], fgate_b[l]])
        bias_n = jnp.pad(gate_bias, (0, LANES - 2 * M_HEADS)).reshape(1, LANES)
        bias_t = gate_bias.reshape(2 * M_HEADS, 1)
        ym = _mlstm(z3, gates_n, gates_t, conv_w[l], row(conv_b[l]), bias_n, bias_t, mlstm_norm_w[l])

        xf = _merge(xf, ya.reshape(n, -1), ym.reshape(n, -1), z, row(gate_b[l]),
                    w_proj_a[l].astype(BF16), w_proj_m[l].astype(BF16), w_out[l].astype(BF16),
                    row(mix_norm_post[l]))

        wgu, wd = _prep_ffn(ffn2_w_gu[l], ffn2_w_down[l])
        xf = _ffn(xf, row(ffn2_norm_pre[l]), wgu, wd, row(ffn2_norm_post[l]))
    return xf.reshape(b, s, d)
```

```python
import functools
import math

import jax
import jax.numpy as jnp
from jax import lax
from jax.experimental import pallas as pl
from jax.experimental.pallas import tpu as pltpu

F32 = jnp.float32
BF16 = jnp.bfloat16

EPS = 1e-6
DEPTH = 2
A_HEADS = 8
A_DHEAD = 64
M_HEADS = 8
M_DQK = 64
M_DV = 128
CONV_K = 4
D_FF = 2816
FFN_SCALE = 0.5

LANES = 128
VMEM_LIMIT_BYTES = 56 * 1024 * 1024

ROW_TILE = 512
FF_CHUNK = 256
PROJ_CHUNK = 512
ATT_TQ = 512
ATT_TK = 512
M_CHUNK = 256
CONV_HALO = 8

_NT = (((1,), (1,)), ((), ()))


def _params(n_axes, flags=None):
    return pltpu.CompilerParams(dimension_semantics=("arbitrary",) * n_axes,
                                vmem_limit_bytes=VMEM_LIMIT_BYTES, flags=flags)


def _resident(shape):
    nd = len(shape)
    return pl.BlockSpec(shape, lambda *_: (0,) * nd, pipeline_mode=pl.Buffered(1))


def _rms(x):
    return x * lax.rsqrt(jnp.mean(x * x, axis=-1, keepdims=True) + EPS)


def _dot(a, b):
    return jnp.dot(a, b, preferred_element_type=F32)


def _dot_nt(a, b):
    return lax.dot_general(a, b, _NT, preferred_element_type=F32)


def _ffn_kernel(x_ref, npre_ref, wgu_ref, wd_ref, npost_ref, o_ref, h_ref, acc_ref, *, n_chunks):
    x = x_ref[...]
    h_ref[...] = (_rms(x) * npre_ref[...]).astype(BF16)
    for j in range(n_chunks):
        h = h_ref[...]
        g = _dot(h, wgu_ref[j])
        u = _dot(h, wgu_ref[n_chunks + j])
        a = (g * jax.nn.sigmoid(g) * u).astype(BF16)
        d = _dot(a, wd_ref[j])
        if j == 0:
            acc_ref[...] = d
        else:
            acc_ref[...] += d
    o_ref[...] = x + FFN_SCALE * (_rms(acc_ref[...]) * npost_ref[...])


def _ffn(x, npre, wgu, wd, npost):
    n, d = x.shape
    n_chunks = wd.shape[0]
    tm = min(ROW_TILE, n)
    return pl.pallas_call(
        functools.partial(_ffn_kernel, n_chunks=n_chunks),
        grid=(n // tm,),
        in_specs=[
            pl.BlockSpec((tm, d), lambda i: (i, 0)),
            _resident(npre.shape),
            _resident(wgu.shape),
            _resident(wd.shape),
            _resident(npost.shape),
        ],
        out_specs=pl.BlockSpec((tm, d), lambda i: (i, 0)),
        out_shape=jax.ShapeDtypeStruct((n, d), F32),
        scratch_shapes=[pltpu.VMEM((tm, d), BF16), pltpu.VMEM((tm, d), F32)],
        compiler_params=_params(1),
        name="ffn",
    )(x, npre, wgu, wd, npost)


def _mixin_kernel(x_ref, nw_ref, w_ref, wg_ref, z_ref, g_ref, *, n_chunks, cw):
    h = (_rms(x_ref[...]) * nw_ref[...]).astype(BF16)
    for j in range(n_chunks):
        z_ref[:, j * cw:(j + 1) * cw] = _dot(h, w_ref[j]).astype(BF16)
    g_ref[...] = _dot(h, wg_ref[...])


def _mixin(x, nw, w, wg):
    n, d = x.shape
    n_chunks, _, cw = w.shape
    tm = min(ROW_TILE, n)
    return pl.pallas_call(
        functools.partial(_mixin_kernel, n_chunks=n_chunks, cw=cw),
        grid=(n // tm,),
        in_specs=[
            pl.BlockSpec((tm, d), lambda i: (i, 0)),
            _resident(nw.shape),
            _resident(w.shape),
            _resident(wg.shape),
        ],
        out_specs=[
            pl.BlockSpec((tm, n_chunks * cw), lambda i: (i, 0)),
            pl.BlockSpec((tm, LANES), lambda i: (i, 0)),
        ],
        out_shape=[
            jax.ShapeDtypeStruct((n, n_chunks * cw), BF16),
            jax.ShapeDtypeStruct((n, LANES), F32),
        ],
        compiler_params=_params(1),
        name="mixin",
    )(x, nw, w, wg)


ONES_ROWS = 16


def _attn_kernel(q_ref, k_ref, v_ref, lam_ref, nw_ref, o_ref,
                 qq_ref, vt_ref, bias_ref, s_ref, p_ref, alpha_ref, m_ref, acc_ref,
                 *, tq, tk, lam_init):
    s_len, dv = v_ref.shape
    nq = s_len // tq
    n_blocks = nq * (nq + 1) // 2
    lane = lax.broadcasted_iota(jnp.int32, (1, LANES), 1)
    first_map = lane < A_DHEAD
    q_scale = (A_DHEAD ** -0.5) * math.log2(math.e)

    ones_rows = (lax.broadcasted_iota(jnp.int32, (ONES_ROWS, tk), 0) == 0).astype(BF16)

    def vt_body(j, c):
        vt_ref[j, 0:dv, :] = v_ref[pl.ds(j * tk, tk), :].astype(F32).T.astype(BF16)
        vt_ref[j, dv:dv + ONES_ROWS, :] = ones_rows
        return c

    lax.fori_loop(0, s_len // tk, vt_body, 0)

    def qq_body(j, c):
        q = (q_ref[pl.ds(j * tq, tq), :].astype(F32) * q_scale).astype(BF16)
        zero = jnp.zeros_like(q)
        qq_ref[j, 0:tq, :] = jnp.where(first_map, q, zero)
        qq_ref[j, tq:2 * tq, :] = jnp.where(first_map, zero, q)
        return c

    lax.fori_loop(0, nq, qq_body, 0)

    key = lax.broadcasted_iota(jnp.int32, (tk, 2 * tq), 0)
    col = lax.broadcasted_iota(jnp.int32, (tk, 2 * tq), 1)
    qry = jnp.where(col >= tq, col - tq, col)
    bias_ref[...] = jnp.where(key <= qry, 0.0, -jnp.inf).astype(F32)

    s_ref[...] = jnp.zeros(s_ref.shape, F32)
    p_ref[...] = jnp.zeros(p_ref.shape, BF16)
    alpha_ref[...] = jnp.zeros(alpha_ref.shape, F32)
    m_ref[...] = jnp.zeros(m_ref.shape, F32)
    acc_ref[...] = jnp.zeros(acc_ref.shape, F32)

    lp = lam_ref[...]
    lam = (jnp.exp(jnp.sum(lp[0:1] * lp[1:2], axis=-1, keepdims=True))
           - jnp.exp(jnp.sum(lp[2:3] * lp[3:4], axis=-1, keepdims=True)) + lam_init)

    def stages(carry, diagonal, slot):
        qa, ka, qb, kb, qc, kc = carry
        other = 1 - slot

        pv = _dot(vt_ref[kc], p_ref[slot])
        acc_ref[slot] = alpha_ref[slot] * acc_ref[other] + pv

        st = s_ref[other]
        if diagonal:
            st = st + bias_ref[...]
        m_old = jnp.where(kb == 0, -jnp.inf, m_ref[...])
        m_new = jnp.maximum(m_old, jnp.max(st, axis=0, keepdims=True))
        p_ref[other] = jnp.exp2(st - m_new).astype(BF16)
        alpha_ref[other] = jnp.exp2(m_old - m_new)
        m_ref[...] = m_new

        s_ref[slot] = _dot_nt(k_ref[pl.ds(ka * tk, tk), :], qq_ref[qa])

        within = ka < qa
        more = qa + 1 < nq
        qa_n = jnp.where(within, qa, jnp.where(more, qa + 1, qa))
        ka_n = jnp.where(within, ka + 1, jnp.where(more, 0, ka))
        return qa_n, ka_n, qa, ka, qb, kb

    def iteration(i, carry, slot):
        _, _, qb, kb, qc, kc = carry
        new_carry = lax.cond(kb == qb,
                             lambda: stages(carry, True, slot),
                             lambda: stages(carry, False, slot))

        @pl.when((kc == qc) & (i >= 2) & (i < n_blocks + 2))
        def _():
            acc = acc_ref[slot]
            l_fin = acc[dv:dv + 1, :]
            o_t = acc[:dv, :tq] / l_fin[:, :tq] - lam * (acc[:dv, tq:] / l_fin[:, tq:])
            o = _rms(o_t.T) * nw_ref[...] * (1.0 - lam_init)
            o_ref[pl.ds(qc * tq, tq), :] = o.astype(o_ref.dtype)

        return new_carry

    def pair(j, carry):
        carry = iteration(2 * j, carry, 0)
        return iteration(2 * j + 1, carry, 1)

    zero_i = jnp.int32(0)
    lax.fori_loop(0, (n_blocks + 3) // 2, pair, (zero_i,) * 6)


def _attention(z3, lam_p, nw, lam_init):
    b, s, _ = z3.shape
    dq = 2 * A_DHEAD
    tq = min(ATT_TQ, s)
    tk = min(ATT_TK, s)
    assert tq == tk
    return pl.pallas_call(
        functools.partial(_attn_kernel, tq=tq, tk=tk, lam_init=lam_init),
        grid=(b, A_HEADS),
        in_specs=[
            pl.BlockSpec((None, s, dq), lambda i, h: (i, 0, h)),
            pl.BlockSpec((None, s, dq), lambda i, h: (i, 0, A_HEADS + h)),
            pl.BlockSpec((None, s, dq), lambda i, h: (i, 0, 2 * A_HEADS + h)),
            _resident(lam_p.shape),
            _resident(nw.shape),
        ],
        out_specs=pl.BlockSpec((None, s, dq), lambda i, h: (i, 0, h)),
        out_shape=jax.ShapeDtypeStruct((b, s, A_HEADS * dq), BF16),
        scratch_shapes=[
            pltpu.VMEM((s // tq, 2 * tq, dq), BF16),
            pltpu.VMEM((s // tk, dq + ONES_ROWS, tk), BF16),
            pltpu.VMEM((tk, 2 * tq), F32),
            pltpu.VMEM((2, tk, 2 * tq), F32),
            pltpu.VMEM((2, tk, 2 * tq), BF16),
            pltpu.VMEM((2, 1, 2 * tq), F32),
            pltpu.VMEM((1, 2 * tq), F32),
            pltpu.VMEM((2, dq + ONES_ROWS, 2 * tq), F32),
        ],
        compiler_params=_params(2),
        name="diff_attn",
    )(z3, z3, z3, lam_p, nw)


def _mlstm_kernel(qk_ref, v_ref, og_ref, gn_ref, gt_ref, cw_ref, cb_ref, bn_ref, bt_ref, nw_ref,
                  o_ref, xbuf_ref, c_ref, m_ref, *, lc):
    nqk = M_HEADS * M_DQK
    c_idx = pl.program_id(1)

    @pl.when(c_idx == 0)
    def _():
        xbuf_ref[0:CONV_HALO, :] = jnp.zeros((CONV_HALO, xbuf_ref.shape[1]), F32)
        c_ref[...] = jnp.zeros(c_ref.shape, F32)
        m_ref[...] = jnp.zeros(m_ref.shape, F32)

    xbuf_ref[CONV_HALO:, :] = qk_ref[...].astype(F32)
    y = cb_ref[...]
    for i in range(CONV_K):
        off = CONV_HALO - (CONV_K - 1) + i
        y = y + cw_ref[i:i + 1, :] * xbuf_ref[off:off + lc, :]
    xbuf_ref[0:CONV_HALO, :] = xbuf_ref[lc:lc + CONV_HALO, :]
    a = y * jax.nn.sigmoid(y)
    q_all = a[:, :nqk] * (M_DQK ** -0.5)
    k_all = a[:, nqk:]

    gn = gn_ref[...] + bn_ref[...]
    logf_n = jax.nn.log_sigmoid(gn)
    r_i = lax.broadcasted_iota(jnp.int32, (lc, lc), 0)
    c_i = lax.broadcasted_iota(jnp.int32, (lc, lc), 1)
    causal = c_i <= r_i
    tri = causal.astype(F32)
    bcum_n = jnp.dot(tri, logf_n, precision=lax.Precision.HIGHEST, preferred_element_type=F32)
    gt = gt_ref[...] + bt_ref[...]
    ig_t = gt[0:M_HEADS]
    logf_t = jax.nn.log_sigmoid(gt[M_HEADS:2 * M_HEADS])
    bcum_t = lax.dot_general(logf_t, tri, _NT, precision=lax.Precision.HIGHEST,
                             preferred_element_type=F32)
    b_last = bcum_t[:, lc - 1:lc]

    lane = lax.broadcasted_iota(jnp.int32, (1, LANES), 1)
    sub = lax.broadcasted_iota(jnp.int32, (LANES, 1), 0)
    ones_col = jnp.broadcast_to((lane == 0).astype(BF16), (lc, LANES))

    for pj in range(M_HEADS // 2):
        qp = q_all[:, pj * LANES:(pj + 1) * LANES]
        kp = k_all[:, pj * LANES:(pj + 1) * LANES]
        kb = kp.astype(BF16)
        kpt = kp.T
        c_old = c_ref[pj]
        c_bf = c_old.astype(BF16)
        row_scale = jnp.zeros((LANES, 1), F32)
        upd = jnp.zeros(c_old.shape, F32)
        for half in range(2):
            h = 2 * pj + half
            in_head_l = (lane >= half * M_DQK) & (lane < (half + 1) * M_DQK)
            in_head_s = (sub >= half * M_DQK) & (sub < (half + 1) * M_DQK)
            qh = jnp.where(in_head_l, qp, 0.0).astype(BF16)
            vext = jnp.concatenate([v_ref[:, h * M_DV:(h + 1) * M_DV], ones_col], axis=1)

            s = _dot_nt(qh, kb)
            bc = bcum_n[:, M_HEADS + h:M_HEADS + h + 1]
            br = bcum_t[h:h + 1, :]
            ir = ig_t[h:h + 1, :]
            dm = jnp.where(causal, bc - br + ir, -jnp.inf)
            m_prev = m_ref[h:h + 1, 0:1]
            m_inter = bc + m_prev
            m_t = jnp.maximum(m_inter, jnp.max(dm, axis=-1, keepdims=True))
            p = (jnp.exp(dm - m_t) * s).astype(BF16)
            sc = jnp.exp(m_inter - m_t)
            tot = sc * _dot(qh, c_bf) + _dot(p, vext)
            num = tot[:, :M_DV]
            den = tot[:, M_DV:M_DV + 1]
            hh = num / jnp.maximum(jnp.abs(den), jnp.exp(-m_t))
            hn = _rms(hh) * nw_ref[h:h + 1, :]
            og = og_ref[:, h * M_DV:(h + 1) * M_DV].astype(F32)
            o_ref[:, h * M_DV:(h + 1) * M_DV] = (jax.nn.sigmoid(og) * hn).astype(o_ref.dtype)

            bl = b_last[h:h + 1, :]
            a_row = bl - br + ir
            m_new = jnp.maximum(bl + m_prev, jnp.max(a_row, axis=-1, keepdims=True))
            w_row = jnp.exp(a_row - m_new)
            sp = jnp.exp(bl + m_prev - m_new)
            kw = jnp.where(in_head_s, kpt * w_row, 0.0).astype(BF16)
            upd = upd + _dot(kw, vext)
            row_scale = jnp.where(in_head_s, sp, row_scale)
            m_ref[h:h + 1, :] = jnp.broadcast_to(m_new, (1, LANES))
        c_ref[pj] = row_scale * c_old + upd


def _mlstm(z3, gates_n, gates_t, cw, cb, bias_n, bias_t, nw):
    b, s, _ = z3.shape
    lc = min(M_CHUNK, s)
    nqk2 = 2 * M_HEADS * M_DQK
    nv = M_HEADS * M_DV
    qk_blk = (3 * A_HEADS * 2 * A_DHEAD) // nqk2
    v_blk = qk_blk + 1
    og_blk = qk_blk + 2
    return pl.pallas_call(
        functools.partial(_mlstm_kernel, lc=lc),
        grid=(b, s // lc),
        in_specs=[
            pl.BlockSpec((None, lc, nqk2), lambda i, c: (i, c, qk_blk)),
            pl.BlockSpec((None, lc, nv), lambda i, c: (i, c, v_blk)),
            pl.BlockSpec((None, lc, nv), lambda i, c: (i, c, og_blk)),
            pl.BlockSpec((None, lc, LANES), lambda i, c: (i, c, 0)),
            pl.BlockSpec((None, 2 * M_HEADS, lc), lambda i, c: (i, 0, c)),
            _resident(cw.shape),
            _resident(cb.shape),
            _resident(bias_n.shape),
            _resident(bias_t.shape),
            _resident(nw.shape),
        ],
        out_specs=pl.BlockSpec((None, lc, nv), lambda i, c: (i, c, 0)),
        out_shape=jax.ShapeDtypeStruct((b, s, nv), BF16),
        scratch_shapes=[
            pltpu.VMEM((lc + CONV_HALO, nqk2), F32),
            pltpu.VMEM((M_HEADS // 2, LANES, M_DV + LANES), F32),
            pltpu.VMEM((M_HEADS, LANES), F32),
        ],
        compiler_params=_params(2),
        name="mlstm",
    )(z3, z3, z3, gates_n, gates_t, cw, cb, bias_n, bias_t, nw)


def _merge_kernel(x_ref, ya_ref, ym_ref, gl_ref, gb_ref, wa_ref, wm_ref, wo_ref, nw_ref, o_ref):
    d = x_ref.shape[1]
    g = jax.nn.sigmoid(gl_ref[...].astype(F32) + gb_ref[...])
    merged = g[:, :d] * _dot(ya_ref[...], wa_ref[...]) + g[:, d:] * _dot(ym_ref[...], wm_ref[...])
    y = _dot(merged.astype(BF16), wo_ref[...])
    o_ref[...] = x_ref[...] + _rms(y) * nw_ref[...]


def _merge(x, ya, ym, z, gb, wa, wm, wo, nw):
    n, d = x.shape
    tm = min(ROW_TILE, n)
    gl_blk = (z.shape[1] - 2 * d) // (2 * d)
    return pl.pallas_call(
        _merge_kernel,
        grid=(n // tm,),
        in_specs=[
            pl.BlockSpec((tm, d), lambda i: (i, 0)),
            pl.BlockSpec((tm, d), lambda i: (i, 0)),
            pl.BlockSpec((tm, d), lambda i: (i, 0)),
            pl.BlockSpec((tm, 2 * d), lambda i: (i, gl_blk)),
            _resident(gb.shape),
            _resident(wa.shape),
            _resident(wm.shape),
            _resident(wo.shape),
            _resident(nw.shape),
        ],
        out_specs=pl.BlockSpec((tm, d), lambda i: (i, 0)),
        out_shape=jax.ShapeDtypeStruct((n, d), F32),
        compiler_params=_params(1),
        name="merge",
    )(x, ya, ym, z, gb, wa, wm, wo, nw)


def _prep_ffn(w_gu, w_down):
    d, two_ff = w_gu.shape
    n_chunks = two_ff // 2 // FF_CHUNK
    wgu = w_gu.astype(BF16).reshape(d, 2 * n_chunks, FF_CHUNK).transpose(1, 0, 2)
    wd = w_down.astype(BF16).reshape(n_chunks, FF_CHUNK, d)
    return wgu, wd


def kernel(x, ffn1_norm_pre, ffn1_w_gu, ffn1_w_down, ffn1_norm_post, mix_norm_pre, w_in,
           attn_lam_q1, attn_lam_k1, attn_lam_q2, attn_lam_k2, attn_norm_w, conv_w, conv_b,
           igate_b, fgate_b, mlstm_norm_w, w_proj_a, w_proj_m, gate_b, w_out, mix_norm_post,
           ffn2_norm_pre, ffn2_w_gu, ffn2_w_down, ffn2_norm_post):
    b, s, d = x.shape
    n = b * s
    a_w = 3 * A_HEADS * 2 * A_DHEAD
    m_w = 2 * M_HEADS * M_DQK + 2 * M_HEADS * M_DV
    g0 = a_w + m_w
    g1 = g0 + 2 * M_HEADS
    row = lambda v: v.reshape(1, -1)

    xf = x.reshape(n, d)
    for l in range(DEPTH):
        wgu, wd = _prep_ffn(ffn1_w_gu[l], ffn1_w_down[l])
        xf = _ffn(xf, row(ffn1_norm_pre[l]), wgu, wd, row(ffn1_norm_post[l]))

        w_main = jnp.concatenate([w_in[l][:, :g0], w_in[l][:, g1:]], axis=1).astype(BF16)
        n_pc = w_main.shape[1] // PROJ_CHUNK
        w_main = w_main.reshape(d, n_pc, PROJ_CHUNK).transpose(1, 0, 2)
        w_gate = jnp.pad(w_in[l][:, g0:g1], ((0, 0), (0, LANES - 2 * M_HEADS))).astype(BF16)
        z, gates = _mixin(xf, row(mix_norm_pre[l]), w_main, w_gate)
        z3 = z.reshape(b, s, -1)

        lam_init = 0.8 - 0.6 * math.exp(-0.3 * l)
        lam_p = jnp.stack([attn_lam_q1[l], attn_lam_k1[l], attn_lam_q2[l], attn_lam_k2[l]])
        ya = _attention(z3, lam_p, row(attn_norm_w[l]), lam_init)

        gates_n = gates.reshape(b, s, LANES)
        gates_t = gates_n[:, :, :2 * M_HEADS].transpose(0, 2, 1)
        gate_bias = jnp.concatenate([igate_b[l], fgate_b[l]])
        bias_n = jnp.pad(gate_bias, (0, LANES - 2 * M_HEADS)).reshape(1, LANES)
        bias_t = gate_bias.reshape(2 * M_HEADS, 1)
        ym = _mlstm(z3, gates_n, gates_t, conv_w[l], row(conv_b[l]), bias_n, bias_t, mlstm_norm_w[l])

        xf = _merge(xf, ya.reshape(n, -1), ym.reshape(n, -1), z, row(gate_b[l]),
                    w_proj_a[l].astype(BF16), w_proj_m[l].astype(BF16), w_out[l].astype(BF16),
                    row(mix_norm_post[l]))

        wgu, wd = _prep_ffn(ffn2_w_gu[l], ffn2_w_down[l])
        xf = _ffn(xf, row(ffn2_norm_pre[l]), wgu, wd, row(ffn2_norm_post[l]))
    return xf.reshape(b, s, d)
```

```python
import functools
import math

import jax
import jax.numpy as jnp
from jax import lax
from jax.experimental import pallas as pl
from jax.experimental.pallas import tpu as pltpu

F32 = jnp.float32
BF16 = jnp.bfloat16

EPS = 1e-6
DEPTH = 2
A_HEADS = 8
A_DHEAD = 64
M_HEADS = 8
M_DQK = 64
M_DV = 128
CONV_K = 4
D_FF = 2816
FFN_SCALE = 0.5

LANES = 128
VMEM_LIMIT_BYTES = 56 * 1024 * 1024

ROW_TILE = 512
FF_CHUNK = 256
PROJ_CHUNK = 512
ATT_TQ = 512
ATT_TK = 512
M_CHUNK = 256
CONV_HALO = 8

_NT = (((1,), (1,)), ((), ()))


def _params(n_axes, flags=None):
    return pltpu.CompilerParams(dimension_semantics=("arbitrary",) * n_axes,
                                vmem_limit_bytes=VMEM_LIMIT_BYTES, flags=flags)


def _resident(shape):
    nd = len(shape)
    return pl.BlockSpec(shape, lambda *_: (0,) * nd, pipeline_mode=pl.Buffered(1))


def _rms(x):
    return x * lax.rsqrt(jnp.mean(x * x, axis=-1, keepdims=True) + EPS)


def _dot(a, b):
    return jnp.dot(a, b, preferred_element_type=F32)


def _dot_nt(a, b):
    return lax.dot_general(a, b, _NT, preferred_element_type=F32)


def _ffn_kernel(x_ref, npre_ref, wgu_ref, wd_ref, npost_ref, o_ref, h_ref, acc_ref, *, n_chunks):
    x = x_ref[...]
    h_ref[...] = (_rms(x) * npre_ref[...]).astype(BF16)
    for j in range(n_chunks):
        h = h_ref[...]
        g = _dot(h, wgu_ref[j])
        u = _dot(h, wgu_ref[n_chunks + j])
        a = (g * jax.nn.sigmoid(g) * u).astype(BF16)
        d = _dot(a, wd_ref[j])
        if j == 0:
            acc_ref[...] = d
        else:
            acc_ref[...] += d
    o_ref[...] = x + FFN_SCALE * (_rms(acc_ref[...]) * npost_ref[...])


def _ffn(x, npre, wgu, wd, npost):
    n, d = x.shape
    n_chunks = wd.shape[0]
    tm = min(ROW_TILE, n)
    return pl.pallas_call(
        functools.partial(_ffn_kernel, n_chunks=n_chunks),
        grid=(n // tm,),
        in_specs=[
            pl.BlockSpec((tm, d), lambda i: (i, 0)),
            _resident(npre.shape),
            _resident(wgu.shape),
            _resident(wd.shape),
            _resident(npost.shape),
        ],
        out_specs=pl.BlockSpec((tm, d), lambda i: (i, 0)),
        out_shape=jax.ShapeDtypeStruct((n, d), F32),
        scratch_shapes=[pltpu.VMEM((tm, d), BF16), pltpu.VMEM((tm, d), F32)],
        compiler_params=_params(1),
        name="ffn",
    )(x, npre, wgu, wd, npost)


def _mixin_kernel(x_ref, nw_ref, w_ref, wg_ref, z_ref, g_ref, *, n_chunks, cw):
    h = (_rms(x_ref[...]) * nw_ref[...]).astype(BF16)
    for j in range(n_chunks):
        z_ref[:, j * cw:(j + 1) * cw] = _dot(h, w_ref[j]).astype(BF16)
    g_ref[...] = _dot(h, wg_ref[...])


def _mixin(x, nw, w, wg):
    n, d = x.shape
    n_chunks, _, cw = w.shape
    tm = min(ROW_TILE, n)
    return pl.pallas_call(
        functools.partial(_mixin_kernel, n_chunks=n_chunks, cw=cw),
        grid=(n // tm,),
        in_specs=[
            pl.BlockSpec((tm, d), lambda i: (i, 0)),
            _resident(nw.shape),
            _resident(w.shape),
            _resident(wg.shape),
        ],
        out_specs=[
            pl.BlockSpec((tm, n_chunks * cw), lambda i: (i, 0)),
            pl.BlockSpec((tm, LANES), lambda i: (i, 0)),
        ],
        out_shape=[
            jax.ShapeDtypeStruct((n, n_chunks * cw), BF16),
            jax.ShapeDtypeStruct((n, LANES), F32),
        ],
        compiler_params=_params(1),
        name="mixin",
    )(x, nw, w, wg)


ONES_ROWS = 16


def _attn_kernel(q_ref, k_ref, v_ref, lam_ref, nw_ref, o_ref,
                 qq_ref, vt_ref, bias_ref, s_ref, cmax_ref, p_ref, alpha_ref, m_ref, acc_ref,
                 *, tq, tk, lam_init):
    s_len, dv = v_ref.shape
    nq = s_len // tq
    n_blocks = nq * (nq + 1) // 2
    lane = lax.broadcasted_iota(jnp.int32, (1, LANES), 1)
    first_map = lane < A_DHEAD
    q_scale = (A_DHEAD ** -0.5) * math.log2(math.e)

    ones_rows = (lax.broadcasted_iota(jnp.int32, (ONES_ROWS, tk), 0) == 0).astype(BF16)

    def vt_body(j, c):
        vt_ref[j, 0:dv, :] = v_ref[pl.ds(j * tk, tk), :].astype(F32).T.astype(BF16)
        vt_ref[j, dv:dv + ONES_ROWS, :] = ones_rows
        return c

    lax.fori_loop(0, s_len // tk, vt_body, 0)

    def qq_body(j, c):
        q = (q_ref[pl.ds(j * tq, tq), :].astype(F32) * q_scale).astype(BF16)
        zero = jnp.zeros_like(q)
        qq_ref[j, 0:tq, :] = jnp.where(first_map, q, zero)
        qq_ref[j, tq:2 * tq, :] = jnp.where(first_map, zero, q)
        return c

    lax.fori_loop(0, nq, qq_body, 0)

    key = lax.broadcasted_iota(jnp.int32, (tk, 2 * tq), 0)
    col = lax.broadcasted_iota(jnp.int32, (tk, 2 * tq), 1)
    qry = jnp.where(col >= tq, col - tq, col)
    bias_ref[...] = jnp.where(key <= qry, 0.0, -jnp.inf).astype(F32)

    s_ref[...] = jnp.zeros(s_ref.shape, F32)
    p_ref[...] = jnp.zeros(p_ref.shape, BF16)
    cmax_ref[...] = jnp.zeros(cmax_ref.shape, F32)
    alpha_ref[...] = jnp.zeros(alpha_ref.shape, F32)
    m_ref[...] = jnp.zeros(m_ref.shape, F32)
    acc_ref[...] = jnp.zeros(acc_ref.shape, F32)

    lp = lam_ref[...]
    lam = (jnp.exp(jnp.sum(lp[0:1] * lp[1:2], axis=-1, keepdims=True))
           - jnp.exp(jnp.sum(lp[2:3] * lp[3:4], axis=-1, keepdims=True)) + lam_init)

    def stages(carry, diagonal, slot):
        qa, ka, qb, kb, qc, kc = carry
        other = 1 - slot

        pv = _dot(vt_ref[kc], p_ref[slot])
        acc_ref[slot] = alpha_ref[slot] * acc_ref[other] + pv

        m_old = jnp.where(kb == 0, -jnp.inf, m_ref[...])
        m_new = jnp.maximum(m_old, cmax_ref[other])
        p_ref[other] = jnp.exp2(s_ref[other] - m_new).astype(BF16)
        alpha_ref[other] = jnp.exp2(m_old - m_new)
        m_ref[...] = m_new

        st = _dot_nt(k_ref[pl.ds(ka * tk, tk), :], qq_ref[qa])
        if diagonal:
            st = st + bias_ref[...]
        s_ref[slot] = st
        cmax_ref[slot] = jnp.max(st, axis=0, keepdims=True)

        within = ka < qa
        more = qa + 1 < nq
        qa_n = jnp.where(within, qa, jnp.where(more, qa + 1, qa))
        ka_n = jnp.where(within, ka + 1, jnp.where(more, 0, ka))
        return qa_n, ka_n, qa, ka, qb, kb

    def iteration(i, carry, slot):
        qa, ka, _, _, qc, kc = carry
        new_carry = lax.cond(ka == qa,
                             lambda: stages(carry, True, slot),
                             lambda: stages(carry, False, slot))

        @pl.when((kc == qc) & (i >= 2) & (i < n_blocks + 2))
        def _():
            acc = acc_ref[slot]
            l_fin = acc[dv:dv + 1, :]
            o_t = acc[:dv, :tq] / l_fin[:, :tq] - lam * (acc[:dv, tq:] / l_fin[:, tq:])
            o = _rms(o_t.T) * nw_ref[...] * (1.0 - lam_init)
            o_ref[pl.ds(qc * tq, tq), :] = o.astype(o_ref.dtype)

        return new_carry

    def pair(j, carry):
        carry = iteration(2 * j, carry, 0)
        return iteration(2 * j + 1, carry, 1)

    zero_i = jnp.int32(0)
    lax.fori_loop(0, (n_blocks + 3) // 2, pair, (zero_i,) * 6)


def _attention(z3, lam_p, nw, lam_init):
    b, s, _ = z3.shape
    dq = 2 * A_DHEAD
    tq = min(ATT_TQ, s)
    tk = min(ATT_TK, s)
    assert tq == tk
    return pl.pallas_call(
        functools.partial(_attn_kernel, tq=tq, tk=tk, lam_init=lam_init),
        grid=(b, A_HEADS),
        in_specs=[
            pl.BlockSpec((None, s, dq), lambda i, h: (i, 0, h)),
            pl.BlockSpec((None, s, dq), lambda i, h: (i, 0, A_HEADS + h)),
            pl.BlockSpec((None, s, dq), lambda i, h: (i, 0, 2 * A_HEADS + h)),
            _resident(lam_p.shape),
            _resident(nw.shape),
        ],
        out_specs=pl.BlockSpec((None, s, dq), lambda i, h: (i, 0, h)),
        out_shape=jax.ShapeDtypeStruct((b, s, A_HEADS * dq), BF16),
        scratch_shapes=[
            pltpu.VMEM((s // tq, 2 * tq, dq), BF16),
            pltpu.VMEM((s // tk, dq + ONES_ROWS, tk), BF16),
            pltpu.VMEM((tk, 2 * tq), F32),
            pltpu.VMEM((2, tk, 2 * tq), F32),
            pltpu.VMEM((2, 1, 2 * tq), F32),
            pltpu.VMEM((2, tk, 2 * tq), BF16),
            pltpu.VMEM((2, 1, 2 * tq), F32),
            pltpu.VMEM((1, 2 * tq), F32),
            pltpu.VMEM((2, dq + ONES_ROWS, 2 * tq), F32),
        ],
        compiler_params=_params(2),
        name="diff_attn",
    )(z3, z3, z3, lam_p, nw)


def _mlstm_kernel(qk_ref, v_ref, og_ref, gn_ref, gt_ref, cw_ref, cb_ref, bn_ref, bt_ref, nw_ref,
                  o_ref, xbuf_ref, c_ref, m_ref, *, lc):
    nqk = M_HEADS * M_DQK
    c_idx = pl.program_id(1)

    @pl.when(c_idx == 0)
    def _():
        xbuf_ref[0:CONV_HALO, :] = jnp.zeros((CONV_HALO, xbuf_ref.shape[1]), F32)
        c_ref[...] = jnp.zeros(c_ref.shape, F32)
        m_ref[...] = jnp.zeros(m_ref.shape, F32)

    xbuf_ref[CONV_HALO:, :] = qk_ref[...].astype(F32)
    y = cb_ref[...]
    for i in range(CONV_K):
        off = CONV_HALO - (CONV_K - 1) + i
        y = y + cw_ref[i:i + 1, :] * xbuf_ref[off:off + lc, :]
    xbuf_ref[0:CONV_HALO, :] = xbuf_ref[lc:lc + CONV_HALO, :]
    a = y * jax.nn.sigmoid(y)
    q_all = a[:, :nqk] * (M_DQK ** -0.5)
    k_all = a[:, nqk:]

    gn = gn_ref[...] + bn_ref[...]
    logf_n = jax.nn.log_sigmoid(gn)
    r_i = lax.broadcasted_iota(jnp.int32, (lc, lc), 0)
    c_i = lax.broadcasted_iota(jnp.int32, (lc, lc), 1)
    causal = c_i <= r_i
    tri = causal.astype(F32)
    bcum_n = jnp.dot(tri, logf_n, precision=lax.Precision.HIGHEST, preferred_element_type=F32)
    gt = gt_ref[...] + bt_ref[...]
    ig_t = gt[0:M_HEADS]
    logf_t = jax.nn.log_sigmoid(gt[M_HEADS:2 * M_HEADS])
    bcum_t = lax.dot_general(logf_t, tri, _NT, precision=lax.Precision.HIGHEST,
                             preferred_element_type=F32)
    b_last = bcum_t[:, lc - 1:lc]

    lane = lax.broadcasted_iota(jnp.int32, (1, LANES), 1)
    sub = lax.broadcasted_iota(jnp.int32, (LANES, 1), 0)
    ones_col = jnp.broadcast_to((lane == 0).astype(BF16), (lc, LANES))

    for pj in range(M_HEADS // 2):
        qp = q_all[:, pj * LANES:(pj + 1) * LANES]
        kp = k_all[:, pj * LANES:(pj + 1) * LANES]
        kb = kp.astype(BF16)
        kpt = kp.T
        c_old = c_ref[pj]
        c_bf = c_old.astype(BF16)
        row_scale = jnp.zeros((LANES, 1), F32)
        upd = jnp.zeros(c_old.shape, F32)
        for half in range(2):
            h = 2 * pj + half
            in_head_l = (lane >= half * M_DQK) & (lane < (half + 1) * M_DQK)
            in_head_s = (sub >= half * M_DQK) & (sub < (half + 1) * M_DQK)
            qh = jnp.where(in_head_l, qp, 0.0).astype(BF16)
            vext = jnp.concatenate([v_ref[:, h * M_DV:(h + 1) * M_DV], ones_col], axis=1)

            s = _dot_nt(qh, kb)
            bc = bcum_n[:, M_HEADS + h:M_HEADS + h + 1]
            br = bcum_t[h:h + 1, :]
            ir = ig_t[h:h + 1, :]
            dm = jnp.where(causal, bc - br + ir, -jnp.inf)
            m_prev = m_ref[h:h + 1, 0:1]
            m_inter = bc + m_prev
            m_t = jnp.maximum(m_inter, jnp.max(dm, axis=-1, keepdims=True))
            p = (jnp.exp(dm - m_t) * s).astype(BF16)
            sc = jnp.exp(m_inter - m_t)
            tot = sc * _dot(qh, c_bf) + _dot(p, vext)
            num = tot[:, :M_DV]
            den = tot[:, M_DV:M_DV + 1]
            hh = num / jnp.maximum(jnp.abs(den), jnp.exp(-m_t))
            hn = _rms(hh) * nw_ref[h:h + 1, :]
            og = og_ref[:, h * M_DV:(h + 1) * M_DV].astype(F32)
            o_ref[:, h * M_DV:(h + 1) * M_DV] = (jax.nn.sigmoid(og) * hn).astype(o_ref.dtype)

            bl = b_last[h:h + 1, :]
            a_row = bl - br + ir
            m_new = jnp.maximum(bl + m_prev, jnp.max(a_row, axis=-1, keepdims=True))
            w_row = jnp.exp(a_row - m_new)
            sp = jnp.exp(bl + m_prev - m_new)
            kw = jnp.where(in_head_s, kpt * w_row, 0.0).astype(BF16)
            upd = upd + _dot(kw, vext)
            row_scale = jnp.where(in_head_s, sp, row_scale)
            m_ref[h:h + 1, :] = jnp.broadcast_to(m_new, (1, LANES))
        c_ref[pj] = row_scale * c_old + upd


def _mlstm(z3, gates_n, gates_t, cw, cb, bias_n, bias_t, nw):
    b, s, _ = z3.shape
    lc = min(M_CHUNK, s)
    nqk2 = 2 * M_HEADS * M_DQK
    nv = M_HEADS * M_DV
    qk_blk = (3 * A_HEADS * 2 * A_DHEAD) // nqk2
    v_blk = qk_blk + 1
    og_blk = qk_blk + 2
    return pl.pallas_call(
        functools.partial(_mlstm_kernel, lc=lc),
        grid=(b, s // lc),
        in_specs=[
            pl.BlockSpec((None, lc, nqk2), lambda i, c: (i, c, qk_blk)),
            pl.BlockSpec((None, lc, nv), lambda i, c: (i, c, v_blk)),
            pl.BlockSpec((None, lc, nv), lambda i, c: (i, c, og_blk)),
            pl.BlockSpec((None, lc, LANES), lambda i, c: (i, c, 0)),
            pl.BlockSpec((None, 2 * M_HEADS, lc), lambda i, c: (i, 0, c)),
            _resident(cw.shape),
            _resident(cb.shape),
            _resident(bias_n.shape),
            _resident(bias_t.shape),
            _resident(nw.shape),
        ],
        out_specs=pl.BlockSpec((None, lc, nv), lambda i, c: (i, c, 0)),
        out_shape=jax.ShapeDtypeStruct((b, s, nv), BF16),
        scratch_shapes=[
            pltpu.VMEM((lc + CONV_HALO, nqk2), F32),
            pltpu.VMEM((M_HEADS // 2, LANES, M_DV + LANES), F32),
            pltpu.VMEM((M_HEADS, LANES), F32),
        ],
        compiler_params=_params(2),
        name="mlstm",
    )(z3, z3, z3, gates_n, gates_t, cw, cb, bias_n, bias_t, nw)


def _merge_kernel(x_ref, ya_ref, ym_ref, gl_ref, gb_ref, wa_ref, wm_ref, wo_ref, nw_ref, o_ref):
    d = x_ref.shape[1]
    g = jax.nn.sigmoid(gl_ref[...].astype(F32) + gb_ref[...])
    merged = g[:, :d] * _dot(ya_ref[...], wa_ref[...]) + g[:, d:] * _dot(ym_ref[...], wm_ref[...])
    y = _dot(merged.astype(BF16), wo_ref[...])
    o_ref[...] = x_ref[...] + _rms(y) * nw_ref[...]


def _merge(x, ya, ym, z, gb, wa, wm, wo, nw):
    n, d = x.shape
    tm = min(ROW_TILE, n)
    gl_blk = (z.shape[1] - 2 * d) // (2 * d)
    return pl.pallas_call(
        _merge_kernel,
        grid=(n // tm,),
        in_specs=[
            pl.BlockSpec((tm, d), lambda i: (i, 0)),
            pl.BlockSpec((tm, d), lambda i: (i, 0)),
            pl.BlockSpec((tm, d), lambda i: (i, 0)),
            pl.BlockSpec((tm, 2 * d), lambda i: (i, gl_blk)),
            _resident(gb.shape),
            _resident(wa.shape),
            _resident(wm.shape),
            _resident(wo.shape),
            _resident(nw.shape),
        ],
        out_specs=pl.BlockSpec((tm, d), lambda i: (i, 0)),
        out_shape=jax.ShapeDtypeStruct((n, d), F32),
        compiler_params=_params(1),
        name="merge",
    )(x, ya, ym, z, gb, wa, wm, wo, nw)


def _prep_ffn(w_gu, w_down):
    d, two_ff = w_gu.shape
    n_chunks = two_ff // 2 // FF_CHUNK
    wgu = w_gu.astype(BF16).reshape(d, 2 * n_chunks, FF_CHUNK).transpose(1, 0, 2)
    wd = w_down.astype(BF16).reshape(n_chunks, FF_CHUNK, d)
    return wgu, wd


def kernel(x, ffn1_norm_pre, ffn1_w_gu, ffn1_w_down, ffn1_norm_post, mix_norm_pre, w_in,
           attn_lam_q1, attn_lam_k1, attn_lam_q2, attn_lam_k2, attn_norm_w, conv_w, conv_b,
           igate_b, fgate_b, mlstm_norm_w, w_proj_a, w_proj_m, gate_b, w_out, mix_norm_post,
           ffn2_norm_pre, ffn2_w_gu, ffn2_w_down, ffn2_norm_post):
    b, s, d = x.shape
    n = b * s
    a_w = 3 * A_HEADS * 2 * A_DHEAD
    m_w = 2 * M_HEADS * M_DQK + 2 * M_HEADS * M_DV
    g0 = a_w + m_w
    g1 = g0 + 2 * M_HEADS
    row = lambda v: v.reshape(1, -1)

    xf = x.reshape(n, d)
    for l in range(DEPTH):
        wgu, wd = _prep_ffn(ffn1_w_gu[l], ffn1_w_down[l])
        xf = _ffn(xf, row(ffn1_norm_pre[l]), wgu, wd, row(ffn1_norm_post[l]))

        w_main = jnp.concatenate([w_in[l][:, :g0], w_in[l][:, g1:]], axis=1).astype(BF16)
        n_pc = w_main.shape[1] // PROJ_CHUNK
        w_main = w_main.reshape(d, n_pc, PROJ_CHUNK).transpose(1, 0, 2)
        w_gate = jnp.pad(w_in[l][:, g0:g1], ((0, 0), (0, LANES - 2 * M_HEADS))).astype(BF16)
        z, gates = _mixin(xf, row(mix_norm_pre[l]), w_main, w_gate)
        z3 = z.reshape(b, s, -1)

        lam_init = 0.8 - 0.6 * math.exp(-0.3 * l)
        lam_p = jnp.stack([attn_lam_q1[l], attn_lam_k1[l], attn_lam_q2[l], attn_lam_k2[l]])
        ya = _attention(z3, lam_p, row(attn_norm_w[l]), lam_init)

        gates_n = gates.reshape(b, s, LANES)
        gates_t = gates_n[:, :, :2 * M_HEADS].transpose(0, 2, 1)
        gate_bias = jnp.concatenate([igate_b[l], fgate_b[l]])
        bias_n = jnp.pad(gate_bias, (0, LANES - 2 * M_HEADS)).reshape(1, LANES)
        bias_t = gate_bias.reshape(2 * M_HEADS, 1)
        ym = _mlstm(z3, gates_n, gates_t, conv_w[l], row(conv_b[l]), bias_n, bias_t, mlstm_norm_w[l])

        xf = _merge(xf, ya.reshape(n, -1), ym.reshape(n, -1), z, row(gate_b[l]),
                    w_proj_a[l].astype(BF16), w_proj_m[l].astype(BF16), w_out[l].astype(BF16),
                    row(mix_norm_post[l]))

        wgu, wd = _prep_ffn(ffn2_w_gu[l], ffn2_w_down[l])
        xf = _ffn(xf, row(ffn2_norm_pre[l]), wgu, wd, row(ffn2_norm_post[l]))
    return xf.reshape(b, s, d)
```

```python
import functools
import math

import jax
import jax.numpy as jnp
from jax import lax
from jax.experimental import pallas as pl
from jax.experimental.pallas import tpu as pltpu

F32 = jnp.float32
BF16 = jnp.bfloat16

EPS = 1e-6
DEPTH = 2
A_HEADS = 8
A_DHEAD = 64
M_HEADS = 8
M_DQK = 64
M_DV = 128
CONV_K = 4
D_FF = 2816
FFN_SCALE = 0.5

LANES = 128
VMEM_LIMIT_BYTES = 56 * 1024 * 1024

ROW_TILE = 1024
ROW_SUB = 512
FF_CHUNK = 256
PROJ_CHUNK = 512
ATT_TQ = 512
ATT_TK = 512
M_CHUNK = 256
CONV_HALO = 8

_NT = (((1,), (1,)), ((), ()))


def _params(n_axes, flags=None):
    return pltpu.CompilerParams(dimension_semantics=("arbitrary",) * n_axes,
                                vmem_limit_bytes=VMEM_LIMIT_BYTES, flags=flags)


def _resident(shape):
    nd = len(shape)
    return pl.BlockSpec(shape, lambda *_: (0,) * nd, pipeline_mode=pl.Buffered(1))


def _rms(x):
    return x * lax.rsqrt(jnp.mean(x * x, axis=-1, keepdims=True) + EPS)


def _dot(a, b):
    return jnp.dot(a, b, preferred_element_type=F32)


def _dot_nt(a, b):
    return lax.dot_general(a, b, _NT, preferred_element_type=F32)


def _ffn_kernel(x_ref, npre_ref, wgu_ref, wd_ref, npost_ref, o_ref, h_ref, acc_ref, *, sub, fc):
    d_ff = wd_ref.shape[0]
    for r in range(x_ref.shape[0] // sub):
        rows = slice(r * sub, (r + 1) * sub)
        h_ref[rows, :] = (_rms(x_ref[rows, :]) * npre_ref[...]).astype(BF16)
        for j in range(d_ff // fc):
            h = h_ref[rows, :]
            g = _dot(h, wgu_ref[:, j * fc:(j + 1) * fc])
            u = _dot(h, wgu_ref[:, d_ff + j * fc:d_ff + (j + 1) * fc])
            a = (g * jax.nn.sigmoid(g) * u).astype(BF16)
            d = _dot(a, wd_ref[j * fc:(j + 1) * fc, :])
            if j == 0:
                acc_ref[rows, :] = d
            else:
                acc_ref[rows, :] += d
        o_ref[rows, :] = x_ref[rows, :] + FFN_SCALE * (_rms(acc_ref[rows, :]) * npost_ref[...])


def _ffn(x, npre, wgu, wd, npost):
    n, d = x.shape
    tm = min(ROW_TILE, n)
    return pl.pallas_call(
        functools.partial(_ffn_kernel, sub=min(ROW_SUB, tm), fc=FF_CHUNK),
        grid=(n // tm,),
        in_specs=[
            pl.BlockSpec((tm, d), lambda i: (i, 0)),
            _resident(npre.shape),
            _resident(wgu.shape),
            _resident(wd.shape),
            _resident(npost.shape),
        ],
        out_specs=pl.BlockSpec((tm, d), lambda i: (i, 0)),
        out_shape=jax.ShapeDtypeStruct((n, d), F32),
        scratch_shapes=[pltpu.VMEM((tm, d), BF16), pltpu.VMEM((tm, d), F32)],
        compiler_params=_params(1),
        name="ffn",
    )(x, npre, wgu, wd, npost)


def _mixin_kernel(x_ref, nw_ref, w_ref, wg_ref, z_ref, g_ref, *, sub, cw):
    for r in range(x_ref.shape[0] // sub):
        rows = slice(r * sub, (r + 1) * sub)
        h = (_rms(x_ref[rows, :]) * nw_ref[...]).astype(BF16)
        for j in range(w_ref.shape[1] // cw):
            z_ref[rows, j * cw:(j + 1) * cw] = _dot(h, w_ref[:, j * cw:(j + 1) * cw]).astype(BF16)
        g_ref[rows, :] = _dot(h, wg_ref[...])


def _mixin(x, nw, w, wg):
    n, d = x.shape
    nz = w.shape[1]
    tm = min(ROW_SUB, n)
    return pl.pallas_call(
        functools.partial(_mixin_kernel, sub=min(ROW_SUB, tm), cw=PROJ_CHUNK),
        grid=(n // tm,),
        in_specs=[
            pl.BlockSpec((tm, d), lambda i: (i, 0)),
            _resident(nw.shape),
            _resident(w.shape),
            _resident(wg.shape),
        ],
        out_specs=[
            pl.BlockSpec((tm, nz), lambda i: (i, 0)),
            pl.BlockSpec((tm, LANES), lambda i: (i, 0)),
        ],
        out_shape=[
            jax.ShapeDtypeStruct((n, nz), BF16),
            jax.ShapeDtypeStruct((n, LANES), F32),
        ],
        compiler_params=_params(1),
        name="mixin",
    )(x, nw, w, wg)


ONES_ROWS = 16


def _attn_kernel(q_ref, k_ref, v_ref, lam_ref, nw_ref, o_ref,
                 qq_ref, vt_ref, bias_ref, s_ref, cmax_ref, p_ref, alpha_ref, m_ref, acc_ref,
                 *, tq, tk, lam_init):
    s_len, dv = v_ref.shape
    nq = s_len // tq
    n_blocks = nq * (nq + 1) // 2
    lane = lax.broadcasted_iota(jnp.int32, (1, LANES), 1)
    first_map = lane < A_DHEAD
    q_scale = (A_DHEAD ** -0.5) * math.log2(math.e)

    ones_rows = (lax.broadcasted_iota(jnp.int32, (ONES_ROWS, tk), 0) == 0).astype(BF16)

    def vt_body(j, c):
        vt_ref[j, 0:dv, :] = v_ref[pl.ds(j * tk, tk), :].astype(F32).T.astype(BF16)
        vt_ref[j, dv:dv + ONES_ROWS, :] = ones_rows
        return c

    lax.fori_loop(0, s_len // tk, vt_body, 0)

    def qq_body(j, c):
        q = (q_ref[pl.ds(j * tq, tq), :].astype(F32) * q_scale).astype(BF16)
        zero = jnp.zeros_like(q)
        qq_ref[j, 0:tq, :] = jnp.where(first_map, q, zero)
        qq_ref[j, tq:2 * tq, :] = jnp.where(first_map, zero, q)
        return c

    lax.fori_loop(0, nq, qq_body, 0)

    @pl.when((pl.program_id(0) == 0) & (pl.program_id(1) == 0))
    def _():
        key = lax.broadcasted_iota(jnp.int32, (tk, 2 * tq), 0)
        col = lax.broadcasted_iota(jnp.int32, (tk, 2 * tq), 1)
        qry = jnp.where(col >= tq, col - tq, col)
        bias_ref[...] = jnp.where(key <= qry, 0.0, -jnp.inf).astype(F32)

        s_ref[...] = jnp.zeros(s_ref.shape, F32)
        p_ref[...] = jnp.zeros(p_ref.shape, BF16)
        cmax_ref[...] = jnp.zeros(cmax_ref.shape, F32)
        alpha_ref[...] = jnp.zeros(alpha_ref.shape, F32)
        m_ref[...] = jnp.zeros(m_ref.shape, F32)
        acc_ref[...] = jnp.zeros(acc_ref.shape, F32)

    lp = lam_ref[...]
    lam = (jnp.exp(jnp.sum(lp[0:1] * lp[1:2], axis=-1, keepdims=True))
           - jnp.exp(jnp.sum(lp[2:3] * lp[3:4], axis=-1, keepdims=True)) + lam_init)

    def stages(carry, diagonal, slot):
        qa, ka, qb, kb, qc, kc = carry
        other = 1 - slot

        pv = _dot(vt_ref[kc], p_ref[slot])
        acc_ref[slot] = alpha_ref[slot] * acc_ref[other] + pv

        m_old = jnp.where(kb == 0, -jnp.inf, m_ref[...])
        m_new = jnp.maximum(m_old, cmax_ref[other])
        p_ref[other] = jnp.exp2(s_ref[other] - m_new).astype(BF16)
        alpha_ref[other] = jnp.exp2(m_old - m_new)
        m_ref[...] = m_new

        st = _dot_nt(k_ref[pl.ds(ka * tk, tk), :], qq_ref[qa])
        if diagonal:
            st = st + bias_ref[...]
        s_ref[slot] = st
        cmax_ref[slot] = jnp.max(st, axis=0, keepdims=True)

        within = ka < qa
        more = qa + 1 < nq
        qa_n = jnp.where(within, qa, jnp.where(more, qa + 1, qa))
        ka_n = jnp.where(within, ka + 1, jnp.where(more, 0, ka))
        return qa_n, ka_n, qa, ka, qb, kb

    def iteration(i, carry, slot):
        qa, ka, _, _, qc, kc = carry
        new_carry = lax.cond(ka == qa,
                             lambda: stages(carry, True, slot),
                             lambda: stages(carry, False, slot))

        @pl.when((kc == qc) & (i >= 2) & (i < n_blocks + 2))
        def _():
            acc = acc_ref[slot]
            l_fin = acc[dv:dv + 1, :]
            o_t = acc[:dv, :tq] / l_fin[:, :tq] - lam * (acc[:dv, tq:] / l_fin[:, tq:])
            o = _rms(o_t.T) * nw_ref[...] * (1.0 - lam_init)
            o_ref[pl.ds(qc * tq, tq), :] = o.astype(o_ref.dtype)

        return new_carry

    def pair(j, carry):
        carry = iteration(2 * j, carry, 0)
        return iteration(2 * j + 1, carry, 1)

    zero_i = jnp.int32(0)
    lax.fori_loop(0, (n_blocks + 3) // 2, pair, (zero_i,) * 6)


def _attention(z3, lam_p, nw, lam_init):
    b, s, _ = z3.shape
    dq = 2 * A_DHEAD
    tq = min(ATT_TQ, s)
    tk = min(ATT_TK, s)
    assert tq == tk
    return pl.pallas_call(
        functools.partial(_attn_kernel, tq=tq, tk=tk, lam_init=lam_init),
        grid=(b, A_HEADS),
        in_specs=[
            pl.BlockSpec((None, s, dq), lambda i, h: (i, 0, h)),
            pl.BlockSpec((None, s, dq), lambda i, h: (i, 0, A_HEADS + h)),
            pl.BlockSpec((None, s, dq), lambda i, h: (i, 0, 2 * A_HEADS + h)),
            _resident(lam_p.shape),
            _resident(nw.shape),
        ],
        out_specs=pl.BlockSpec((None, s, dq), lambda i, h: (i, 0, h)),
        out_shape=jax.ShapeDtypeStruct((b, s, A_HEADS * dq), BF16),
        scratch_shapes=[
            pltpu.VMEM((s // tq, 2 * tq, dq), BF16),
            pltpu.VMEM((s // tk, dq + ONES_ROWS, tk), BF16),
            pltpu.VMEM((tk, 2 * tq), F32),
            pltpu.VMEM((2, tk, 2 * tq), F32),
            pltpu.VMEM((2, 1, 2 * tq), F32),
            pltpu.VMEM((2, tk, 2 * tq), BF16),
            pltpu.VMEM((2, 1, 2 * tq), F32),
            pltpu.VMEM((1, 2 * tq), F32),
            pltpu.VMEM((2, dq + ONES_ROWS, 2 * tq), F32),
        ],
        compiler_params=_params(2),
        name="diff_attn",
    )(z3, z3, z3, lam_p, nw)


def _mlstm_kernel(qk_ref, v_ref, og_ref, gn_ref, gt_ref, cw_ref, cb_ref, bn_ref, bt_ref, nw_ref,
                  o_ref, xbuf_ref, c_ref, m_ref, *, lc):
    nqk = M_HEADS * M_DQK
    c_idx = pl.program_id(1)

    @pl.when(c_idx == 0)
    def _():
        xbuf_ref[0:CONV_HALO, :] = jnp.zeros((CONV_HALO, xbuf_ref.shape[1]), F32)
        c_ref[...] = jnp.zeros(c_ref.shape, F32)
        m_ref[...] = jnp.zeros(m_ref.shape, F32)

    xbuf_ref[CONV_HALO:, :] = qk_ref[...].astype(F32)
    y = cb_ref[...]
    for i in range(CONV_K):
        off = CONV_HALO - (CONV_K - 1) + i
        y = y + cw_ref[i:i + 1, :] * xbuf_ref[off:off + lc, :]
    xbuf_ref[0:CONV_HALO, :] = xbuf_ref[lc:lc + CONV_HALO, :]
    a = y * jax.nn.sigmoid(y)
    q_all = a[:, :nqk] * (M_DQK ** -0.5)
    k_all = a[:, nqk:]

    gn = gn_ref[...] + bn_ref[...]
    logf_n = jax.nn.log_sigmoid(gn)
    r_i = lax.broadcasted_iota(jnp.int32, (lc, lc), 0)
    c_i = lax.broadcasted_iota(jnp.int32, (lc, lc), 1)
    causal = c_i <= r_i
    tri = causal.astype(F32)
    bcum_n = jnp.dot(tri, logf_n, precision=lax.Precision.HIGHEST, preferred_element_type=F32)
    gt = gt_ref[...] + bt_ref[...]
    ig_t = gt[0:M_HEADS]
    logf_t = jax.nn.log_sigmoid(gt[M_HEADS:2 * M_HEADS])
    bcum_t = lax.dot_general(logf_t, tri, _NT, precision=lax.Precision.HIGHEST,
                             preferred_element_type=F32)
    b_last = bcum_t[:, lc - 1:lc]

    lane = lax.broadcasted_iota(jnp.int32, (1, LANES), 1)
    sub = lax.broadcasted_iota(jnp.int32, (LANES, 1), 0)
    ones_col = jnp.broadcast_to((lane == 0).astype(BF16), (lc, LANES))

    for pj in range(M_HEADS // 2):
        qp = q_all[:, pj * LANES:(pj + 1) * LANES]
        kp = k_all[:, pj * LANES:(pj + 1) * LANES]
        kb = kp.astype(BF16)
        kpt = kp.T
        c_old = c_ref[pj]
        c_bf = c_old.astype(BF16)
        row_scale = jnp.zeros((LANES, 1), F32)
        upd = jnp.zeros(c_old.shape, F32)
        for half in range(2):
            h = 2 * pj + half
            in_head_l = (lane >= half * M_DQK) & (lane < (half + 1) * M_DQK)
            in_head_s = (sub >= half * M_DQK) & (sub < (half + 1) * M_DQK)
            qh = jnp.where(in_head_l, qp, 0.0).astype(BF16)
            vext = jnp.concatenate([v_ref[:, h * M_DV:(h + 1) * M_DV], ones_col], axis=1)

            s = _dot_nt(qh, kb)
            bc = bcum_n[:, M_HEADS + h:M_HEADS + h + 1]
            br = bcum_t[h:h + 1, :]
            ir = ig_t[h:h + 1, :]
            dm = jnp.where(causal, bc - br + ir, -jnp.inf)
            m_prev = m_ref[h:h + 1, 0:1]
            m_inter = bc + m_prev
            m_t = jnp.maximum(m_inter, jnp.max(dm, axis=-1, keepdims=True))
            p = (jnp.exp(dm - m_t) * s).astype(BF16)
            sc = jnp.exp(m_inter - m_t)
            tot = sc * _dot(qh, c_bf) + _dot(p, vext)
            num = tot[:, :M_DV]
            den = tot[:, M_DV:M_DV + 1]
            hh = num / jnp.maximum(jnp.abs(den), jnp.exp(-m_t))
            hn = _rms(hh) * nw_ref[h:h + 1, :]
            og = og_ref[:, h * M_DV:(h + 1) * M_DV].astype(F32)
            o_ref[:, h * M_DV:(h + 1) * M_DV] = (jax.nn.sigmoid(og) * hn).astype(o_ref.dtype)

            bl = b_last[h:h + 1, :]
            a_row = bl - br + ir
            m_new = jnp.maximum(bl + m_prev, jnp.max(a_row, axis=-1, keepdims=True))
            w_row = jnp.exp(a_row - m_new)
            sp = jnp.exp(bl + m_prev - m_new)
            kw = jnp.where(in_head_s, kpt * w_row, 0.0).astype(BF16)
            upd = upd + _dot(kw, vext)
            row_scale = jnp.where(in_head_s, sp, row_scale)
            m_ref[h:h + 1, :] = jnp.broadcast_to(m_new, (1, LANES))
        c_ref[pj] = row_scale * c_old + upd


def _mlstm(z3, gates_n, gates_t, cw, cb, bias_n, bias_t, nw):
    b, s, _ = z3.shape
    lc = min(M_CHUNK, s)
    nqk2 = 2 * M_HEADS * M_DQK
    nv = M_HEADS * M_DV
    qk_blk = (3 * A_HEADS * 2 * A_DHEAD) // nqk2
    v_blk = qk_blk + 1
    og_blk = qk_blk + 2
    return pl.pallas_call(
        functools.partial(_mlstm_kernel, lc=lc),
        grid=(b, s // lc),
        in_specs=[
            pl.BlockSpec((None, lc, nqk2), lambda i, c: (i, c, qk_blk)),
            pl.BlockSpec((None, lc, nv), lambda i, c: (i, c, v_blk)),
            pl.BlockSpec((None, lc, nv), lambda i, c: (i, c, og_blk)),
            pl.BlockSpec((None, lc, LANES), lambda i, c: (i, c, 0)),
            pl.BlockSpec((None, 2 * M_HEADS, lc), lambda i, c: (i, 0, c)),
            _resident(cw.shape),
            _resident(cb.shape),
            _resident(bias_n.shape),
            _resident(bias_t.shape),
            _resident(nw.shape),
        ],
        out_specs=pl.BlockSpec((None, lc, nv), lambda i, c: (i, c, 0)),
        out_shape=jax.ShapeDtypeStruct((b, s, nv), BF16),
        scratch_shapes=[
            pltpu.VMEM((lc + CONV_HALO, nqk2), F32),
            pltpu.VMEM((M_HEADS // 2, LANES, M_DV + LANES), F32),
            pltpu.VMEM((M_HEADS, LANES), F32),
        ],
        compiler_params=_params(2),
        name="mlstm",
    )(z3, z3, z3, gates_n, gates_t, cw, cb, bias_n, bias_t, nw)


def _merge_kernel(x_ref, ya_ref, ym_ref, gl_ref, gb_ref, wa_ref, wm_ref, wo_ref, nw_ref, o_ref):
    d = x_ref.shape[1]
    g = jax.nn.sigmoid(gl_ref[...].astype(F32) + gb_ref[...])
    merged = g[:, :d] * _dot(ya_ref[...], wa_ref[...]) + g[:, d:] * _dot(ym_ref[...], wm_ref[...])
    y = _dot(merged.astype(BF16), wo_ref[...])
    o_ref[...] = x_ref[...] + _rms(y) * nw_ref[...]


def _merge(x, ya, ym, z, gb, wa, wm, wo, nw):
    n, d = x.shape
    tm = min(ROW_SUB, n)
    gl_blk = (z.shape[1] - 2 * d) // (2 * d)
    return pl.pallas_call(
        _merge_kernel,
        grid=(n // tm,),
        in_specs=[
            pl.BlockSpec((tm, d), lambda i: (i, 0)),
            pl.BlockSpec((tm, d), lambda i: (i, 0)),
            pl.BlockSpec((tm, d), lambda i: (i, 0)),
            pl.BlockSpec((tm, 2 * d), lambda i: (i, gl_blk)),
            _resident(gb.shape),
            _resident(wa.shape),
            _resident(wm.shape),
            _resident(wo.shape),
            _resident(nw.shape),
        ],
        out_specs=pl.BlockSpec((tm, d), lambda i: (i, 0)),
        out_shape=jax.ShapeDtypeStruct((n, d), F32),
        compiler_params=_params(1),
        name="merge",
    )(x, ya, ym, z, gb, wa, wm, wo, nw)


def kernel(x, ffn1_norm_pre, ffn1_w_gu, ffn1_w_down, ffn1_norm_post, mix_norm_pre, w_in,
           attn_lam_q1, attn_lam_k1, attn_lam_q2, attn_lam_k2, attn_norm_w, conv_w, conv_b,
           igate_b, fgate_b, mlstm_norm_w, w_proj_a, w_proj_m, gate_b, w_out, mix_norm_post,
           ffn2_norm_pre, ffn2_w_gu, ffn2_w_down, ffn2_norm_post):
    b, s, d = x.shape
    n = b * s
    a_w = 3 * A_HEADS * 2 * A_DHEAD
    m_w = 2 * M_HEADS * M_DQK + 2 * M_HEADS * M_DV
    g0 = a_w + m_w
    g1 = g0 + 2 * M_HEADS
    row = lambda v: v.reshape(1, -1)

    xf = x.reshape(n, d)
    for l in range(DEPTH):
        wgu, wd = ffn1_w_gu[l].astype(BF16), ffn1_w_down[l].astype(BF16)
        xf = _ffn(xf, row(ffn1_norm_pre[l]), wgu, wd, row(ffn1_norm_post[l]))

        w_main = jnp.concatenate([w_in[l][:, :g0], w_in[l][:, g1:]], axis=1).astype(BF16)
        w_gate = jnp.pad(w_in[l][:, g0:g1], ((0, 0), (0, LANES - 2 * M_HEADS))).astype(BF16)
        z, gates = _mixin(xf, row(mix_norm_pre[l]), w_main, w_gate)
        z3 = z.reshape(b, s, -1)

        lam_init = 0.8 - 0.6 * math.exp(-0.3 * l)
        lam_p = jnp.stack([attn_lam_q1[l], attn_lam_k1[l], attn_lam_q2[l], attn_lam_k2[l]])
        ya = _attention(z3, lam_p, row(attn_norm_w[l]), lam_init)

        gates_n = gates.reshape(b, s, LANES)
        gates_t = gates_n[:, :, :2 * M_HEADS].transpose(0, 2, 1)
        gate_bias = jnp.concatenate([igate_b[l], fgate_b[l]])
        bias_n = jnp.pad(gate_bias, (0, LANES - 2 * M_HEADS)).reshape(1, LANES)
        bias_t = gate_bias.reshape(2 * M_HEADS, 1)
        ym = _mlstm(z3, gates_n, gates_t, conv_w[l], row(conv_b[l]), bias_n, bias_t, mlstm_norm_w[l])

        xf = _merge(xf, ya.reshape(n, -1), ym.reshape(n, -1), z, row(gate_b[l]),
                    w_proj_a[l].astype(BF16), w_proj_m[l].astype(BF16), w_out[l].astype(BF16),
                    row(mix_norm_post[l]))

        wgu, wd = ffn2_w_gu[l].astype(BF16), ffn2_w_down[l].astype(BF16)
        xf = _ffn(xf, row(ffn2_norm_pre[l]), wgu, wd, row(ffn2_norm_post[l]))
    return xf.reshape(b, s, d)
```

```python
import functools
import math

import jax
import jax.numpy as jnp
from jax import lax
from jax.experimental import pallas as pl
from jax.experimental.pallas import tpu as pltpu

F32 = jnp.float32
BF16 = jnp.bfloat16

EPS = 1e-6
DEPTH = 2
A_HEADS = 8
A_DHEAD = 64
M_HEADS = 8
M_DQK = 64
M_DV = 128
CONV_K = 4
D_FF = 2816
FFN_SCALE = 0.5

LANES = 128
VMEM_LIMIT_BYTES = 56 * 1024 * 1024

ROW_TILE = 1024
ROW_SUB = 512
FF_CHUNK = 256
PROJ_CHUNK = 512
ATT_TQ = 512
ATT_TK = 512
M_CHUNK = 256
CONV_HALO = 8

_NT = (((1,), (1,)), ((), ()))


def _params(n_axes, flags=None):
    return pltpu.CompilerParams(dimension_semantics=("arbitrary",) * n_axes,
                                vmem_limit_bytes=VMEM_LIMIT_BYTES, flags=flags)


def _resident(shape):
    nd = len(shape)
    return pl.BlockSpec(shape, lambda *_: (0,) * nd, pipeline_mode=pl.Buffered(1))


def _rms(x):
    return x * lax.rsqrt(jnp.mean(x * x, axis=-1, keepdims=True) + EPS)


def _dot(a, b):
    return jnp.dot(a, b, preferred_element_type=F32)


def _dot_nt(a, b):
    return lax.dot_general(a, b, _NT, preferred_element_type=F32)


def _ffn_kernel(x_ref, npre_ref, wgu_ref, wd_ref, npost_ref, o_ref, h_ref, acc_ref, *, sub, fc):
    d_ff = wd_ref.shape[0]
    for r in range(x_ref.shape[0] // sub):
        rows = slice(r * sub, (r + 1) * sub)
        h_ref[rows, :] = (_rms(x_ref[rows, :]) * npre_ref[...]).astype(BF16)
        for j in range(d_ff // fc):
            h = h_ref[rows, :]
            g = _dot(h, wgu_ref[:, j * fc:(j + 1) * fc])
            u = _dot(h, wgu_ref[:, d_ff + j * fc:d_ff + (j + 1) * fc])
            a = (g * jax.nn.sigmoid(g) * u).astype(BF16)
            d = _dot(a, wd_ref[j * fc:(j + 1) * fc, :])
            if j == 0:
                acc_ref[rows, :] = d
            else:
                acc_ref[rows, :] += d
        o_ref[rows, :] = x_ref[rows, :] + FFN_SCALE * (_rms(acc_ref[rows, :]) * npost_ref[...])


def _ffn(x, npre, wgu, wd, npost):
    n, d = x.shape
    tm = min(ROW_TILE, n)
    return pl.pallas_call(
        functools.partial(_ffn_kernel, sub=min(ROW_SUB, tm), fc=FF_CHUNK),
        grid=(n // tm,),
        in_specs=[
            pl.BlockSpec((tm, d), lambda i: (i, 0)),
            _resident(npre.shape),
            _resident(wgu.shape),
            _resident(wd.shape),
            _resident(npost.shape),
        ],
        out_specs=pl.BlockSpec((tm, d), lambda i: (i, 0)),
        out_shape=jax.ShapeDtypeStruct((n, d), F32),
        scratch_shapes=[pltpu.VMEM((tm, d), BF16), pltpu.VMEM((tm, d), F32)],
        compiler_params=_params(1),
        name="ffn",
    )(x, npre, wgu, wd, npost)


def _mixin_kernel(x_ref, nw_ref, w_ref, wg_ref, z_ref, g_ref, *, sub, cw):
    for r in range(x_ref.shape[0] // sub):
        rows = slice(r * sub, (r + 1) * sub)
        h = (_rms(x_ref[rows, :]) * nw_ref[...]).astype(BF16)
        for j in range(w_ref.shape[1] // cw):
            z_ref[rows, j * cw:(j + 1) * cw] = _dot(h, w_ref[:, j * cw:(j + 1) * cw]).astype(BF16)
        g_ref[rows, :] = _dot(h, wg_ref[...])


def _mixin(x, nw, w, wg):
    n, d = x.shape
    nz = w.shape[1]
    tm = min(ROW_SUB, n)
    return pl.pallas_call(
        functools.partial(_mixin_kernel, sub=min(ROW_SUB, tm), cw=PROJ_CHUNK),
        grid=(n // tm,),
        in_specs=[
            pl.BlockSpec((tm, d), lambda i: (i, 0)),
            _resident(nw.shape),
            _resident(w.shape),
            _resident(wg.shape),
        ],
        out_specs=[
            pl.BlockSpec((tm, nz), lambda i: (i, 0)),
            pl.BlockSpec((tm, LANES), lambda i: (i, 0)),
        ],
        out_shape=[
            jax.ShapeDtypeStruct((n, nz), BF16),
            jax.ShapeDtypeStruct((n, LANES), F32),
        ],
        compiler_params=_params(1),
        name="mixin",
    )(x, nw, w, wg)


ONES_ROWS = 16


def _attn_kernel(q_ref, k_ref, v_ref, lam_ref, nw_ref, o_ref,
                 qq_ref, vt_ref, bias_ref, s_ref, cmax_ref, p_ref, alpha_ref, m_ref, acc_ref,
                 *, tq, tk, lam_init):
    s_len, dv = v_ref.shape
    nq = s_len // tq
    n_blocks = nq * (nq + 1) // 2
    lane = lax.broadcasted_iota(jnp.int32, (1, LANES), 1)
    first_map = lane < A_DHEAD
    q_scale = (A_DHEAD ** -0.5) * math.log2(math.e)

    ones_rows = (lax.broadcasted_iota(jnp.int32, (ONES_ROWS, tk), 0) == 0).astype(BF16)

    def vt_body(j, c):
        vt_ref[j, 0:dv, :] = v_ref[pl.ds(j * tk, tk), :].astype(F32).T.astype(BF16)
        vt_ref[j, dv:dv + ONES_ROWS, :] = ones_rows
        return c

    lax.fori_loop(0, s_len // tk, vt_body, 0)

    def qq_body(j, c):
        q = (q_ref[pl.ds(j * tq, tq), :].astype(F32) * q_scale).astype(BF16)
        zero = jnp.zeros_like(q)
        qq_ref[j, 0:tq, :] = jnp.where(first_map, q, zero)
        qq_ref[j, tq:2 * tq, :] = jnp.where(first_map, zero, q)
        return c

    lax.fori_loop(0, nq, qq_body, 0)

    @pl.when((pl.program_id(0) == 0) & (pl.program_id(1) == 0))
    def _():
        key = lax.broadcasted_iota(jnp.int32, (tk, 2 * tq), 0)
        col = lax.broadcasted_iota(jnp.int32, (tk, 2 * tq), 1)
        qry = jnp.where(col >= tq, col - tq, col)
        bias_ref[...] = jnp.where(key <= qry, 0.0, -jnp.inf).astype(F32)

        s_ref[...] = jnp.zeros(s_ref.shape, F32)
        p_ref[...] = jnp.zeros(p_ref.shape, BF16)
        cmax_ref[...] = jnp.zeros(cmax_ref.shape, F32)
        alpha_ref[...] = jnp.zeros(alpha_ref.shape, F32)
        m_ref[...] = jnp.zeros(m_ref.shape, F32)
        acc_ref[...] = jnp.zeros(acc_ref.shape, F32)

    lp = lam_ref[...]
    lam = (jnp.exp(jnp.sum(lp[0:1] * lp[1:2], axis=-1, keepdims=True))
           - jnp.exp(jnp.sum(lp[2:3] * lp[3:4], axis=-1, keepdims=True)) + lam_init)

    def stages(carry, diagonal, slot):
        qa, ka, qb, kb, qc, kc = carry
        other = 1 - slot

        pv = _dot(vt_ref[kc], p_ref[slot])
        acc_ref[slot] = alpha_ref[slot] * acc_ref[other] + pv

        m_old = jnp.where(kb == 0, -jnp.inf, m_ref[...])
        m_new = jnp.maximum(m_old, cmax_ref[other])
        p_ref[other] = jnp.exp2(s_ref[other] - m_new).astype(BF16)
        alpha_ref[other] = jnp.exp2(m_old - m_new)
        m_ref[...] = m_new

        st = _dot_nt(k_ref[pl.ds(ka * tk, tk), :], qq_ref[qa])
        if diagonal:
            st = st + bias_ref[...]
        s_ref[slot] = st
        cmax_ref[slot] = jnp.max(st, axis=0, keepdims=True)

        within = ka < qa
        more = qa + 1 < nq
        qa_n = jnp.where(within, qa, jnp.where(more, qa + 1, qa))
        ka_n = jnp.where(within, ka + 1, jnp.where(more, 0, ka))
        return qa_n, ka_n, qa, ka, qb, kb

    def iteration(i, carry, slot):
        qa, ka, _, _, qc, kc = carry
        new_carry = lax.cond(ka == qa,
                             lambda: stages(carry, True, slot),
                             lambda: stages(carry, False, slot))

        @pl.when((kc == qc) & (i >= 2) & (i < n_blocks + 2))
        def _():
            acc = acc_ref[slot]
            l_fin = acc[dv:dv + 1, :]
            o_t = acc[:dv, :tq] / l_fin[:, :tq] - lam * (acc[:dv, tq:] / l_fin[:, tq:])
            o = _rms(o_t.T) * nw_ref[...] * (1.0 - lam_init)
            o_ref[pl.ds(qc * tq, tq), :] = o.astype(o_ref.dtype)

        return new_carry

    def pair(j, carry):
        carry = iteration(2 * j, carry, 0)
        return iteration(2 * j + 1, carry, 1)

    zero_i = jnp.int32(0)
    lax.fori_loop(0, (n_blocks + 3) // 2, pair, (zero_i,) * 6)


def _attention(z3, lam_p, nw, lam_init):
    b, s, _ = z3.shape
    dq = 2 * A_DHEAD
    tq = min(ATT_TQ, s)
    tk = min(ATT_TK, s)
    assert tq == tk
    return pl.pallas_call(
        functools.partial(_attn_kernel, tq=tq, tk=tk, lam_init=lam_init),
        grid=(b, A_HEADS),
        in_specs=[
            pl.BlockSpec((None, s, dq), lambda i, h: (i, 0, h)),
            pl.BlockSpec((None, s, dq), lambda i, h: (i, 0, A_HEADS + h)),
            pl.BlockSpec((None, s, dq), lambda i, h: (i, 0, 2 * A_HEADS + h)),
            _resident(lam_p.shape),
            _resident(nw.shape),
        ],
        out_specs=pl.BlockSpec((None, s, dq), lambda i, h: (i, 0, h)),
        out_shape=jax.ShapeDtypeStruct((b, s, A_HEADS * dq), BF16),
        scratch_shapes=[
            pltpu.VMEM((s // tq, 2 * tq, dq), BF16),
            pltpu.VMEM((s // tk, dq + ONES_ROWS, tk), BF16),
            pltpu.VMEM((tk, 2 * tq), F32),
            pltpu.VMEM((2, tk, 2 * tq), F32),
            pltpu.VMEM((2, 1, 2 * tq), F32),
            pltpu.VMEM((2, tk, 2 * tq), BF16),
            pltpu.VMEM((2, 1, 2 * tq), F32),
            pltpu.VMEM((1, 2 * tq), F32),
            pltpu.VMEM((2, dq + ONES_ROWS, 2 * tq), F32),
        ],
        compiler_params=_params(2),
        name="diff_attn",
    )(z3, z3, z3, lam_p, nw)


def _mlstm_kernel(qk_ref, v_ref, og_ref, gn_ref, gt_ref, cw_ref, cb_ref, bn_ref, bt_ref, nw_ref,
                  o_ref, xbuf_ref, c_ref, m_ref, *, lc):
    nqk = M_HEADS * M_DQK
    c_idx = pl.program_id(1)

    @pl.when(c_idx == 0)
    def _():
        xbuf_ref[0:CONV_HALO, :] = jnp.zeros((CONV_HALO, xbuf_ref.shape[1]), F32)
        c_ref[...] = jnp.zeros(c_ref.shape, F32)
        m_ref[...] = jnp.zeros(m_ref.shape, F32)

    xbuf_ref[CONV_HALO:, :] = qk_ref[...].astype(F32)
    y = cb_ref[...]
    for i in range(CONV_K):
        off = CONV_HALO - (CONV_K - 1) + i
        y = y + cw_ref[i:i + 1, :] * xbuf_ref[off:off + lc, :]
    xbuf_ref[0:CONV_HALO, :] = xbuf_ref[lc:lc + CONV_HALO, :]
    a = y * jax.nn.sigmoid(y)
    q_all = a[:, :nqk] * (M_DQK ** -0.5)
    k_all = a[:, nqk:]

    gn = gn_ref[...] + bn_ref[...]
    logf_n = jax.nn.log_sigmoid(gn)
    r_i = lax.broadcasted_iota(jnp.int32, (lc, lc), 0)
    c_i = lax.broadcasted_iota(jnp.int32, (lc, lc), 1)
    causal = c_i <= r_i
    tri = causal.astype(F32)
    bcum_n = jnp.dot(tri, logf_n, precision=lax.Precision.HIGHEST, preferred_element_type=F32)
    gt = gt_ref[...] + bt_ref[...]
    ig_t = gt[0:M_HEADS]
    logf_t = jax.nn.log_sigmoid(gt[M_HEADS:2 * M_HEADS])
    bcum_t = lax.dot_general(logf_t, tri, _NT, precision=lax.Precision.HIGHEST,
                             preferred_element_type=F32)
    b_last = bcum_t[:, lc - 1:lc]

    lane = lax.broadcasted_iota(jnp.int32, (1, LANES), 1)
    sub = lax.broadcasted_iota(jnp.int32, (LANES, 1), 0)
    ones_blk = jnp.ones((lc, LANES), BF16)
    rep = lc // LANES

    for pj in range(M_HEADS // 2):
        qp = q_all[:, pj * LANES:(pj + 1) * LANES]
        kp = k_all[:, pj * LANES:(pj + 1) * LANES]
        kb = kp.astype(BF16)
        kpt = kp.T
        c_old = c_ref[pj]
        c_bf = c_old.astype(BF16)
        row_scale = jnp.zeros((LANES, 1), F32)
        upd = jnp.zeros(c_old.shape, F32)
        for half in range(2):
            h = 2 * pj + half
            in_head_l = (lane >= half * M_DQK) & (lane < (half + 1) * M_DQK)
            in_head_s = (sub >= half * M_DQK) & (sub < (half + 1) * M_DQK)
            qh = jnp.where(in_head_l, qp, 0.0).astype(BF16)
            vext = jnp.concatenate([v_ref[:, h * M_DV:(h + 1) * M_DV], ones_blk], axis=1)

            s = _dot_nt(qh, kb)
            bc = jnp.broadcast_to(bcum_n[:, M_HEADS + h:M_HEADS + h + 1], (lc, LANES))
            br = bcum_t[h:h + 1, :]
            ir = ig_t[h:h + 1, :]
            dm = jnp.where(causal, jnp.concatenate([bc] * rep, axis=1) - br + ir, -jnp.inf)
            m_prev = m_ref[h:h + 1, 0:1]
            m_inter = bc + m_prev
            m_t = jnp.maximum(m_inter, jnp.max(dm, axis=-1, keepdims=True))
            p = (jnp.exp(dm - jnp.concatenate([m_t] * rep, axis=1)) * s).astype(BF16)
            sc = jnp.exp(m_inter - m_t)
            tot = jnp.concatenate([sc, sc], axis=1) * _dot(qh, c_bf) + _dot(p, vext)
            num = tot[:, :M_DV]
            den = tot[:, M_DV:]
            hh = num / jnp.maximum(jnp.abs(den), jnp.exp(-m_t))
            hn = _rms(hh) * nw_ref[h:h + 1, :]
            og = og_ref[:, h * M_DV:(h + 1) * M_DV].astype(F32)
            o_ref[:, h * M_DV:(h + 1) * M_DV] = (jax.nn.sigmoid(og) * hn).astype(o_ref.dtype)

            bl = b_last[h:h + 1, :]
            a_row = bl - br + ir
            m_new = jnp.maximum(bl + m_prev, jnp.max(a_row, axis=-1, keepdims=True))
            w_row = jnp.exp(a_row - m_new)
            sp = jnp.exp(bl + m_prev - m_new)
            kw = jnp.where(in_head_s, kpt * w_row, 0.0).astype(BF16)
            upd = upd + _dot(kw, vext)
            row_scale = jnp.where(in_head_s, sp, row_scale)
            m_ref[h:h + 1, :] = jnp.broadcast_to(m_new, (1, LANES))
        c_ref[pj] = row_scale * c_old + upd


def _mlstm(z3, gates_n, gates_t, cw, cb, bias_n, bias_t, nw):
    b, s, _ = z3.shape
    lc = min(M_CHUNK, s)
    nqk2 = 2 * M_HEADS * M_DQK
    nv = M_HEADS * M_DV
    qk_blk = (3 * A_HEADS * 2 * A_DHEAD) // nqk2
    v_blk = qk_blk + 1
    og_blk = qk_blk + 2
    return pl.pallas_call(
        functools.partial(_mlstm_kernel, lc=lc),
        grid=(b, s // lc),
        in_specs=[
            pl.BlockSpec((None, lc, nqk2), lambda i, c: (i, c, qk_blk)),
            pl.BlockSpec((None, lc, nv), lambda i, c: (i, c, v_blk)),
            pl.BlockSpec((None, lc, nv), lambda i, c: (i, c, og_blk)),
            pl.BlockSpec((None, lc, LANES), lambda i, c: (i, c, 0)),
            pl.BlockSpec((None, 2 * M_HEADS, lc), lambda i, c: (i, 0, c)),
            _resident(cw.shape),
            _resident(cb.shape),
            _resident(bias_n.shape),
            _resident(bias_t.shape),
            _resident(nw.shape),
        ],
        out_specs=pl.BlockSpec((None, lc, nv), lambda i, c: (i, c, 0)),
        out_shape=jax.ShapeDtypeStruct((b, s, nv), BF16),
        scratch_shapes=[
            pltpu.VMEM((lc + CONV_HALO, nqk2), F32),
            pltpu.VMEM((M_HEADS // 2, LANES, M_DV + LANES), F32),
            pltpu.VMEM((M_HEADS, LANES), F32),
        ],
        compiler_params=_params(2),
        name="mlstm",
    )(z3, z3, z3, gates_n, gates_t, cw, cb, bias_n, bias_t, nw)


def _merge_kernel(x_ref, ya_ref, ym_ref, gl_ref, gb_ref, wa_ref, wm_ref, wo_ref, nw_ref, o_ref):
    d = x_ref.shape[1]
    g = jax.nn.sigmoid(gl_ref[...].astype(F32) + gb_ref[...])
    merged = g[:, :d] * _dot(ya_ref[...], wa_ref[...]) + g[:, d:] * _dot(ym_ref[...], wm_ref[...])
    y = _dot(merged.astype(BF16), wo_ref[...])
    o_ref[...] = x_ref[...] + _rms(y) * nw_ref[...]


def _merge(x, ya, ym, z, gb, wa, wm, wo, nw):
    n, d = x.shape
    tm = min(ROW_SUB, n)
    gl_blk = (z.shape[1] - 2 * d) // (2 * d)
    return pl.pallas_call(
        _merge_kernel,
        grid=(n // tm,),
        in_specs=[
            pl.BlockSpec((tm, d), lambda i: (i, 0)),
            pl.BlockSpec((tm, d), lambda i: (i, 0)),
            pl.BlockSpec((tm, d), lambda i: (i, 0)),
            pl.BlockSpec((tm, 2 * d), lambda i: (i, gl_blk)),
            _resident(gb.shape),
            _resident(wa.shape),
            _resident(wm.shape),
            _resident(wo.shape),
            _resident(nw.shape),
        ],
        out_specs=pl.BlockSpec((tm, d), lambda i: (i, 0)),
        out_shape=jax.ShapeDtypeStruct((n, d), F32),
        compiler_params=_params(1),
        name="merge",
    )(x, ya, ym, z, gb, wa, wm, wo, nw)


def kernel(x, ffn1_norm_pre, ffn1_w_gu, ffn1_w_down, ffn1_norm_post, mix_norm_pre, w_in,
           attn_lam_q1, attn_lam_k1, attn_lam_q2, attn_lam_k2, attn_norm_w, conv_w, conv_b,
           igate_b, fgate_b, mlstm_norm_w, w_proj_a, w_proj_m, gate_b, w_out, mix_norm_post,
           ffn2_norm_pre, ffn2_w_gu, ffn2_w_down, ffn2_norm_post):
    b, s, d = x.shape
    n = b * s
    a_w = 3 * A_HEADS * 2 * A_DHEAD
    m_w = 2 * M_HEADS * M_DQK + 2 * M_HEADS * M_DV
    g0 = a_w + m_w
    g1 = g0 + 2 * M_HEADS
    row = lambda v: v.reshape(1, -1)

    xf = x.reshape(n, d)
    for l in range(DEPTH):
        wgu, wd = ffn1_w_gu[l].astype(BF16), ffn1_w_down[l].astype(BF16)
        xf = _ffn(xf, row(ffn1_norm_pre[l]), wgu, wd, row(ffn1_norm_post[l]))

        w_main = jnp.concatenate([w_in[l][:, :g0], w_in[l][:, g1:]], axis=1).astype(BF16)
        w_gate = jnp.pad(w_in[l][:, g0:g1], ((0, 0), (0, LANES - 2 * M_HEADS))).astype(BF16)
        z, gates = _mixin(xf, row(mix_norm_pre[l]), w_main, w_gate)
        z3 = z.reshape(b, s, -1)

        lam_init = 0.8 - 0.6 * math.exp(-0.3 * l)
        lam_p = jnp.stack([attn_lam_q1[l], attn_lam_k1[l], attn_lam_q2[l], attn_lam_k2[l]])
        ya = _attention(z3, lam_p, row(attn_norm_w[l]), lam_init)

        gates_n = gates.reshape(b, s, LANES)
        gates_t = gates_n[:, :, :2 * M_HEADS].transpose(0, 2, 1)
        gate_bias = jnp.concatenate([igate_b[l], fgate_b[l]])
        bias_n = jnp.pad(gate_bias, (0, LANES - 2 * M_HEADS)).reshape(1, LANES)
        bias_t = gate_bias.reshape(2 * M_HEADS, 1)
        ym = _mlstm(z3, gates_n, gates_t, conv_w[l], row(conv_b[l]), bias_n, bias_t, mlstm_norm_w[l])

        xf = _merge(xf, ya.reshape(n, -1), ym.reshape(n, -1), z, row(gate_b[l]),
                    w_proj_a[l].astype(BF16), w_proj_m[l].astype(BF16), w_out[l].astype(BF16),
                    row(mix_norm_post[l]))

        wgu, wd = ffn2_w_gu[l].astype(BF16), ffn2_w_down[l].astype(BF16)
        xf = _ffn(xf, row(ffn2_norm_pre[l]), wgu, wd, row(ffn2_norm_post[l]))
    return xf.reshape(b, s, d)
```

```python
import functools
import math

import jax
import jax.numpy as jnp
from jax import lax
from jax.experimental import pallas as pl
from jax.experimental.pallas import tpu as pltpu

F32 = jnp.float32
BF16 = jnp.bfloat16

EPS = 1e-6
DEPTH = 2
A_HEADS = 8
A_DHEAD = 64
M_HEADS = 8
M_DQK = 64
M_DV = 128
CONV_K = 4
D_FF = 2816
FFN_SCALE = 0.5

LANES = 128
VMEM_LIMIT_BYTES = 56 * 1024 * 1024
LOG2E = math.log2(math.e)

ROW_TILE = 1024
ROW_SUB = 512
FF_CHUNK = 256
PROJ_CHUNK = 512
ATT_TQ = 512
ATT_TK = 512
M_CHUNK = 256
CONV_HALO = 8

_NT = (((1,), (1,)), ((), ()))


def _params(n_axes, flags=None):
    return pltpu.CompilerParams(dimension_semantics=("arbitrary",) * n_axes,
                                vmem_limit_bytes=VMEM_LIMIT_BYTES, flags=flags)


def _resident(shape):
    nd = len(shape)
    return pl.BlockSpec(shape, lambda *_: (0,) * nd, pipeline_mode=pl.Buffered(1))


def _rms(x):
    return x * lax.rsqrt(jnp.mean(x * x, axis=-1, keepdims=True) + EPS)


def _dot(a, b):
    return jnp.dot(a, b, preferred_element_type=F32)


def _dot_nt(a, b):
    return lax.dot_general(a, b, _NT, preferred_element_type=F32)


def _ffn_kernel(x_ref, npre_ref, wgu_ref, wd_ref, npost_ref, o_ref, h_ref, acc_ref, *, sub, fc):
    d_ff = wd_ref.shape[0]
    for r in range(x_ref.shape[0] // sub):
        rows = slice(r * sub, (r + 1) * sub)
        h_ref[rows, :] = (_rms(x_ref[rows, :]) * npre_ref[...]).astype(BF16)
        for j in range(d_ff // fc):
            h = h_ref[rows, :]
            g = _dot(h, wgu_ref[:, j * fc:(j + 1) * fc])
            u = _dot(h, wgu_ref[:, d_ff + j * fc:d_ff + (j + 1) * fc])
            a = (g * jax.nn.sigmoid(g) * u).astype(BF16)
            d = _dot(a, wd_ref[j * fc:(j + 1) * fc, :])
            if j == 0:
                acc_ref[rows, :] = d
            else:
                acc_ref[rows, :] += d
        o_ref[rows, :] = x_ref[rows, :] + FFN_SCALE * (_rms(acc_ref[rows, :]) * npost_ref[...])


def _ffn(x, npre, wgu, wd, npost):
    n, d = x.shape
    tm = min(ROW_TILE, n)
    return pl.pallas_call(
        functools.partial(_ffn_kernel, sub=min(ROW_SUB, tm), fc=FF_CHUNK),
        grid=(n // tm,),
        in_specs=[
            pl.BlockSpec((tm, d), lambda i: (i, 0)),
            _resident(npre.shape),
            _resident(wgu.shape),
            _resident(wd.shape),
            _resident(npost.shape),
        ],
        out_specs=pl.BlockSpec((tm, d), lambda i: (i, 0)),
        out_shape=jax.ShapeDtypeStruct((n, d), F32),
        scratch_shapes=[pltpu.VMEM((tm, d), BF16), pltpu.VMEM((tm, d), F32)],
        compiler_params=_params(1),
        name="ffn",
    )(x, npre, wgu, wd, npost)


def _mixin_kernel(x_ref, nw_ref, w_ref, wg_ref, z_ref, g_ref, *, sub, cw):
    for r in range(x_ref.shape[0] // sub):
        rows = slice(r * sub, (r + 1) * sub)
        h = (_rms(x_ref[rows, :]) * nw_ref[...]).astype(BF16)
        for j in range(w_ref.shape[1] // cw):
            z_ref[rows, j * cw:(j + 1) * cw] = _dot(h, w_ref[:, j * cw:(j + 1) * cw]).astype(BF16)
        g_ref[rows, :] = _dot(h, wg_ref[...])


def _mixin(x, nw, w, wg):
    n, d = x.shape
    nz = w.shape[1]
    tm = min(ROW_SUB, n)
    return pl.pallas_call(
        functools.partial(_mixin_kernel, sub=min(ROW_SUB, tm), cw=PROJ_CHUNK),
        grid=(n // tm,),
        in_specs=[
            pl.BlockSpec((tm, d), lambda i: (i, 0)),
            _resident(nw.shape),
            _resident(w.shape),
            _resident(wg.shape),
        ],
        out_specs=[
            pl.BlockSpec((tm, nz), lambda i: (i, 0)),
            pl.BlockSpec((tm, LANES), lambda i: (i, 0)),
        ],
        out_shape=[
            jax.ShapeDtypeStruct((n, nz), BF16),
            jax.ShapeDtypeStruct((n, LANES), F32),
        ],
        compiler_params=_params(1),
        name="mixin",
    )(x, nw, w, wg)


ONES_ROWS = 16


def _attn_kernel(q_ref, k_ref, v_ref, lam_ref, nw_ref, o_ref,
                 qq_ref, vt_ref, bias_ref, s_ref, cmax_ref, p_ref, alpha_ref, m_ref, acc_ref,
                 *, tq, tk, lam_init):
    s_len, dv = v_ref.shape
    nq = s_len // tq
    n_blocks = nq * (nq + 1) // 2
    lane = lax.broadcasted_iota(jnp.int32, (1, LANES), 1)
    first_map = lane < A_DHEAD
    q_scale = (A_DHEAD ** -0.5) * LOG2E

    ones_rows = (lax.broadcasted_iota(jnp.int32, (ONES_ROWS, tk), 0) == 0).astype(BF16)

    def vt_body(j, c):
        vt_ref[j, 0:dv, :] = v_ref[pl.ds(j * tk, tk), :].astype(F32).T.astype(BF16)
        vt_ref[j, dv:dv + ONES_ROWS, :] = ones_rows
        return c

    lax.fori_loop(0, s_len // tk, vt_body, 0)

    def qq_body(j, c):
        q = (q_ref[pl.ds(j * tq, tq), :].astype(F32) * q_scale).astype(BF16)
        zero = jnp.zeros_like(q)
        qq_ref[j, 0:tq, :] = jnp.where(first_map, q, zero)
        qq_ref[j, tq:2 * tq, :] = jnp.where(first_map, zero, q)
        return c

    lax.fori_loop(0, nq, qq_body, 0)

    @pl.when((pl.program_id(0) == 0) & (pl.program_id(1) == 0))
    def _():
        key = lax.broadcasted_iota(jnp.int32, (tk, 2 * tq), 0)
        col = lax.broadcasted_iota(jnp.int32, (tk, 2 * tq), 1)
        qry = jnp.where(col >= tq, col - tq, col)
        bias_ref[...] = jnp.where(key <= qry, 0.0, -jnp.inf).astype(F32)

        s_ref[...] = jnp.zeros(s_ref.shape, F32)
        p_ref[...] = jnp.zeros(p_ref.shape, BF16)
        cmax_ref[...] = jnp.zeros(cmax_ref.shape, F32)
        alpha_ref[...] = jnp.zeros(alpha_ref.shape, F32)
        m_ref[...] = jnp.zeros(m_ref.shape, F32)
        acc_ref[...] = jnp.zeros(acc_ref.shape, F32)

    lp = lam_ref[...]
    lam = (jnp.exp(jnp.sum(lp[0:1] * lp[1:2], axis=-1, keepdims=True))
           - jnp.exp(jnp.sum(lp[2:3] * lp[3:4], axis=-1, keepdims=True)) + lam_init)

    def stages(carry, diagonal, slot):
        qa, ka, qb, kb, qc, kc = carry
        other = 1 - slot

        pv = _dot(vt_ref[kc], p_ref[slot])
        acc_ref[slot] = alpha_ref[slot] * acc_ref[other] + pv

        m_old = jnp.where(kb == 0, -jnp.inf, m_ref[...])
        m_new = jnp.maximum(m_old, cmax_ref[other])
        p_ref[other] = jnp.exp2(s_ref[other] - m_new).astype(BF16)
        alpha_ref[other] = jnp.exp2(m_old - m_new)
        m_ref[...] = m_new

        st = _dot_nt(k_ref[pl.ds(ka * tk, tk), :], qq_ref[qa])
        if diagonal:
            st = st + bias_ref[...]
        s_ref[slot] = st
        cmax_ref[slot] = jnp.max(st, axis=0, keepdims=True)

        within = ka < qa
        more = qa + 1 < nq
        qa_n = jnp.where(within, qa, jnp.where(more, qa + 1, qa))
        ka_n = jnp.where(within, ka + 1, jnp.where(more, 0, ka))
        return qa_n, ka_n, qa, ka, qb, kb

    def iteration(i, carry, slot):
        qa, ka, _, _, qc, kc = carry
        new_carry = lax.cond(ka == qa,
                             lambda: stages(carry, True, slot),
                             lambda: stages(carry, False, slot))

        @pl.when((kc == qc) & (i >= 2) & (i < n_blocks + 2))
        def _():
            acc = acc_ref[slot]
            l_fin = acc[dv:dv + 1, :]
            inv_l = 1.0 / l_fin
            o_t = acc[:dv, :tq] * inv_l[:, :tq] - lam * (acc[:dv, tq:] * inv_l[:, tq:])
            r = lax.rsqrt(jnp.mean(o_t * o_t, axis=0, keepdims=True) + EPS)
            o = (o_t * r).T * nw_ref[...] * (1.0 - lam_init)
            o_ref[pl.ds(qc * tq, tq), :] = o.astype(o_ref.dtype)

        return new_carry

    def pair(j, carry):
        carry = iteration(2 * j, carry, 0)
        return iteration(2 * j + 1, carry, 1)

    zero_i = jnp.int32(0)
    lax.fori_loop(0, (n_blocks + 3) // 2, pair, (zero_i,) * 6)


def _attention(z3, lam_p, nw, lam_init):
    b, s, _ = z3.shape
    dq = 2 * A_DHEAD
    tq = min(ATT_TQ, s)
    tk = min(ATT_TK, s)
    assert tq == tk
    return pl.pallas_call(
        functools.partial(_attn_kernel, tq=tq, tk=tk, lam_init=lam_init),
        grid=(b, A_HEADS),
        in_specs=[
            pl.BlockSpec((None, s, dq), lambda i, h: (i, 0, h)),
            pl.BlockSpec((None, s, dq), lambda i, h: (i, 0, A_HEADS + h)),
            pl.BlockSpec((None, s, dq), lambda i, h: (i, 0, 2 * A_HEADS + h)),
            _resident(lam_p.shape),
            _resident(nw.shape),
        ],
        out_specs=pl.BlockSpec((None, s, dq), lambda i, h: (i, 0, h)),
        out_shape=jax.ShapeDtypeStruct((b, s, A_HEADS * dq), BF16),
        scratch_shapes=[
            pltpu.VMEM((s // tq, 2 * tq, dq), BF16),
            pltpu.VMEM((s // tk, dq + ONES_ROWS, tk), BF16),
            pltpu.VMEM((tk, 2 * tq), F32),
            pltpu.VMEM((2, tk, 2 * tq), F32),
            pltpu.VMEM((2, 1, 2 * tq), F32),
            pltpu.VMEM((2, tk, 2 * tq), BF16),
            pltpu.VMEM((2, 1, 2 * tq), F32),
            pltpu.VMEM((1, 2 * tq), F32),
            pltpu.VMEM((2, dq + ONES_ROWS, 2 * tq), F32),
        ],
        compiler_params=_params(2),
        name="diff_attn",
    )(z3, z3, z3, lam_p, nw)


def _mlstm_kernel(qk_ref, v_ref, og_ref, gn_ref, gt_ref, cw_ref, cb_ref, bn_ref, bt_ref, nw_ref,
                  o_ref, xbuf_ref, c_ref, m_ref, *, lc):
    nqk = M_HEADS * M_DQK
    c_idx = pl.program_id(1)

    @pl.when(c_idx == 0)
    def _():
        xbuf_ref[0:CONV_HALO, :] = jnp.zeros((CONV_HALO, xbuf_ref.shape[1]), F32)
        c_ref[...] = jnp.zeros(c_ref.shape, F32)
        m_ref[...] = jnp.zeros(m_ref.shape, F32)

    xbuf_ref[CONV_HALO:, :] = qk_ref[...].astype(F32)
    y = cb_ref[...]
    for i in range(CONV_K):
        off = CONV_HALO - (CONV_K - 1) + i
        y = y + cw_ref[i:i + 1, :] * xbuf_ref[off:off + lc, :]
    xbuf_ref[0:CONV_HALO, :] = xbuf_ref[lc:lc + CONV_HALO, :]
    a = y * jax.nn.sigmoid(y)
    q_all = a[:, :nqk] * (M_DQK ** -0.5)
    k_all = a[:, nqk:]

    gn = gn_ref[...] + bn_ref[...]
    logf_n = jax.nn.log_sigmoid(gn) * LOG2E
    r_i = lax.broadcasted_iota(jnp.int32, (lc, lc), 0)
    c_i = lax.broadcasted_iota(jnp.int32, (lc, lc), 1)
    causal = c_i <= r_i
    tri = causal.astype(F32)
    bcum_n = jnp.dot(tri, logf_n, precision=lax.Precision.HIGHEST, preferred_element_type=F32)
    gt = gt_ref[...] + bt_ref[...]
    ig_t = gt[0:M_HEADS] * LOG2E
    logf_t = jax.nn.log_sigmoid(gt[M_HEADS:2 * M_HEADS]) * LOG2E
    bcum_t = lax.dot_general(logf_t, tri, _NT, precision=lax.Precision.HIGHEST,
                             preferred_element_type=F32)
    b_last = bcum_t[:, lc - 1:lc]

    lane = lax.broadcasted_iota(jnp.int32, (1, LANES), 1)
    sub = lax.broadcasted_iota(jnp.int32, (LANES, 1), 0)
    ones_blk = jnp.ones((lc, LANES), BF16)
    rep = lc // LANES

    for pj in range(M_HEADS // 2):
        qp = q_all[:, pj * LANES:(pj + 1) * LANES]
        kp = k_all[:, pj * LANES:(pj + 1) * LANES]
        kb = kp.astype(BF16)
        kpt = kp.T
        c_old = c_ref[pj]
        c_bf = c_old.astype(BF16)
        row_scale = jnp.zeros((LANES, 1), F32)
        upd = jnp.zeros(c_old.shape, F32)
        for half in range(2):
            h = 2 * pj + half
            in_head_l = (lane >= half * M_DQK) & (lane < (half + 1) * M_DQK)
            in_head_s = (sub >= half * M_DQK) & (sub < (half + 1) * M_DQK)
            qh = jnp.where(in_head_l, qp, 0.0).astype(BF16)
            vext = jnp.concatenate([v_ref[:, h * M_DV:(h + 1) * M_DV], ones_blk], axis=1)

            s = _dot_nt(qh, kb)
            bc = jnp.broadcast_to(bcum_n[:, M_HEADS + h:M_HEADS + h + 1], (lc, LANES))
            br = bcum_t[h:h + 1, :]
            ir = ig_t[h:h + 1, :]
            dm = jnp.where(causal, jnp.concatenate([bc] * rep, axis=1) - br + ir, -jnp.inf)
            m_prev = m_ref[h:h + 1, 0:1]
            m_inter = bc + m_prev
            m_t = jnp.maximum(m_inter, jnp.max(dm, axis=-1, keepdims=True))
            p = (jnp.exp2(dm - jnp.concatenate([m_t] * rep, axis=1)) * s).astype(BF16)
            sc = jnp.exp2(m_inter - m_t)
            tot = jnp.concatenate([sc, sc], axis=1) * _dot(qh, c_bf) + _dot(p, vext)
            num = tot[:, :M_DV]
            den = tot[:, M_DV:]
            hh = num / jnp.maximum(jnp.abs(den), jnp.exp2(-m_t))
            hn = _rms(hh) * nw_ref[h:h + 1, :]
            og = og_ref[:, h * M_DV:(h + 1) * M_DV].astype(F32)
            o_ref[:, h * M_DV:(h + 1) * M_DV] = (jax.nn.sigmoid(og) * hn).astype(o_ref.dtype)

            bl = b_last[h:h + 1, :]
            a_row = bl - br + ir
            m_new = jnp.maximum(bl + m_prev, jnp.max(a_row, axis=-1, keepdims=True))
            w_row = jnp.exp2(a_row - m_new)
            sp = jnp.exp2(bl + m_prev - m_new)
            kw = jnp.where(in_head_s, kpt * w_row, 0.0).astype(BF16)
            upd = upd + _dot(kw, vext)
            row_scale = jnp.where(in_head_s, sp, row_scale)
            m_ref[h:h + 1, :] = jnp.broadcast_to(m_new, (1, LANES))
        c_ref[pj] = row_scale * c_old + upd


def _mlstm(z3, gates_n, gates_t, cw, cb, bias_n, bias_t, nw):
    b, s, _ = z3.shape
    lc = min(M_CHUNK, s)
    nqk2 = 2 * M_HEADS * M_DQK
    nv = M_HEADS * M_DV
    qk_blk = (3 * A_HEADS * 2 * A_DHEAD) // nqk2
    v_blk = qk_blk + 1
    og_blk = qk_blk + 2
    return pl.pallas_call(
        functools.partial(_mlstm_kernel, lc=lc),
        grid=(b, s // lc),
        in_specs=[
            pl.BlockSpec((None, lc, nqk2), lambda i, c: (i, c, qk_blk)),
            pl.BlockSpec((None, lc, nv), lambda i, c: (i, c, v_blk)),
            pl.BlockSpec((None, lc, nv), lambda i, c: (i, c, og_blk)),
            pl.BlockSpec((None, lc, LANES), lambda i, c: (i, c, 0)),
            pl.BlockSpec((None, 2 * M_HEADS, lc), lambda i, c: (i, 0, c)),
            _resident(cw.shape),
            _resident(cb.shape),
            _resident(bias_n.shape),
            _resident(bias_t.shape),
            _resident(nw.shape),
        ],
        out_specs=pl.BlockSpec((None, lc, nv), lambda i, c: (i, c, 0)),
        out_shape=jax.ShapeDtypeStruct((b, s, nv), BF16),
        scratch_shapes=[
            pltpu.VMEM((lc + CONV_HALO, nqk2), F32),
            pltpu.VMEM((M_HEADS // 2, LANES, M_DV + LANES), F32),
            pltpu.VMEM((M_HEADS, LANES), F32),
        ],
        compiler_params=_params(2),
        name="mlstm",
    )(z3, z3, z3, gates_n, gates_t, cw, cb, bias_n, bias_t, nw)


def _merge_kernel(x_ref, ya_ref, ym_ref, gl_ref, gb_ref, wa_ref, wm_ref, wo_ref, nw_ref, o_ref):
    d = x_ref.shape[1]
    g = jax.nn.sigmoid(gl_ref[...].astype(F32) + gb_ref[...])
    merged = g[:, :d] * _dot(ya_ref[...], wa_ref[...]) + g[:, d:] * _dot(ym_ref[...], wm_ref[...])
    y = _dot(merged.astype(BF16), wo_ref[...])
    o_ref[...] = x_ref[...] + _rms(y) * nw_ref[...]


def _merge(x, ya, ym, z, gb, wa, wm, wo, nw):
    n, d = x.shape
    tm = min(ROW_SUB, n)
    gl_blk = (z.shape[1] - 2 * d) // (2 * d)
    return pl.pallas_call(
        _merge_kernel,
        grid=(n // tm,),
        in_specs=[
            pl.BlockSpec((tm, d), lambda i: (i, 0)),
            pl.BlockSpec((tm, d), lambda i: (i, 0)),
            pl.BlockSpec((tm, d), lambda i: (i, 0)),
            pl.BlockSpec((tm, 2 * d), lambda i: (i, gl_blk)),
            _resident(gb.shape),
            _resident(wa.shape),
            _resident(wm.shape),
            _resident(wo.shape),
            _resident(nw.shape),
        ],
        out_specs=pl.BlockSpec((tm, d), lambda i: (i, 0)),
        out_shape=jax.ShapeDtypeStruct((n, d), F32),
        compiler_params=_params(1),
        name="merge",
    )(x, ya, ym, z, gb, wa, wm, wo, nw)


def kernel(x, ffn1_norm_pre, ffn1_w_gu, ffn1_w_down, ffn1_norm_post, mix_norm_pre, w_in,
           attn_lam_q1, attn_lam_k1, attn_lam_q2, attn_lam_k2, attn_norm_w, conv_w, conv_b,
           igate_b, fgate_b, mlstm_norm_w, w_proj_a, w_proj_m, gate_b, w_out, mix_norm_post,
           ffn2_norm_pre, ffn2_w_gu, ffn2_w_down, ffn2_norm_post):
    b, s, d = x.shape
    n = b * s
    a_w = 3 * A_HEADS * 2 * A_DHEAD
    m_w = 2 * M_HEADS * M_DQK + 2 * M_HEADS * M_DV
    g0 = a_w + m_w
    g1 = g0 + 2 * M_HEADS
    row = lambda v: v.reshape(1, -1)

    xf = x.reshape(n, d)
    for l in range(DEPTH):
        wgu, wd = ffn1_w_gu[l].astype(BF16), ffn1_w_down[l].astype(BF16)
        xf = _ffn(xf, row(ffn1_norm_pre[l]), wgu, wd, row(ffn1_norm_post[l]))

        w_main = jnp.concatenate([w_in[l][:, :g0], w_in[l][:, g1:]], axis=1).astype(BF16)
        w_gate = jnp.pad(w_in[l][:, g0:g1], ((0, 0), (0, LANES - 2 * M_HEADS))).astype(BF16)
        z, gates = _mixin(xf, row(mix_norm_pre[l]), w_main, w_gate)
        z3 = z.reshape(b, s, -1)

        lam_init = 0.8 - 0.6 * math.exp(-0.3 * l)
        lam_p = jnp.stack([attn_lam_q1[l], attn_lam_k1[l], attn_lam_q2[l], attn_lam_k2[l]])
        ya = _attention(z3, lam_p, row(attn_norm_w[l]), lam_init)

        gates_n = gates.reshape(b, s, LANES)
        gates_t = gates_n[:, :, :2 * M_HEADS].transpose(0, 2, 1)
        gate_bias = jnp.concatenate([igate_b[l], fgate_b[l]])
        bias_n = jnp.pad(gate_bias, (0, LANES - 2 * M_HEADS)).reshape(1, LANES)
        bias_t = gate_bias.reshape(2 * M_HEADS, 1)
        ym = _mlstm(z3, gates_n, gates_t, conv_w[l], row(conv_b[l]), bias_n, bias_t, mlstm_norm_w[l])

        xf = _merge(xf, ya.reshape(n, -1), ym.reshape(n, -1), z, row(gate_b[l]),
                    w_proj_a[l].astype(BF16), w_proj_m[l].astype(BF16), w_out[l].astype(BF16),
                    row(mix_norm_post[l]))

        wgu, wd = ffn2_w_gu[l].astype(BF16), ffn2_w_down[l].astype(BF16)
        xf = _ffn(xf, row(ffn2_norm_pre[l]), wgu, wd, row(ffn2_norm_post[l]))
    return xf.reshape(b, s, d)
```

```python
import functools
import math

import jax
import jax.numpy as jnp
from jax import lax
from jax.experimental import pallas as pl
from jax.experimental.pallas import tpu as pltpu

F32 = jnp.float32
BF16 = jnp.bfloat16

EPS = 1e-6
DEPTH = 2
A_HEADS = 8
A_DHEAD = 64
M_HEADS = 8
M_DQK = 64
M_DV = 128
CONV_K = 4
D_FF = 2816
FFN_SCALE = 0.5

LANES = 128
VMEM_LIMIT_BYTES = 56 * 1024 * 1024
LOG2E = math.log2(math.e)

ROW_TILE = 1024
ROW_SUB = 512
FF_CHUNK = 256
PROJ_CHUNK = 512
ATT_TQ = 512
ATT_TK = 512
ATT_HEADS_PER_STEP = 2
M_CHUNK = 256
CONV_HALO = 8

_NT = (((1,), (1,)), ((), ()))


def _params(n_axes, flags=None):
    return pltpu.CompilerParams(dimension_semantics=("arbitrary",) * n_axes,
                                vmem_limit_bytes=VMEM_LIMIT_BYTES, flags=flags)


def _resident(shape):
    nd = len(shape)
    return pl.BlockSpec(shape, lambda *_: (0,) * nd, pipeline_mode=pl.Buffered(1))


def _rms(x):
    return x * lax.rsqrt(jnp.mean(x * x, axis=-1, keepdims=True) + EPS)


def _dot(a, b):
    return jnp.dot(a, b, preferred_element_type=F32)


def _dot_nt(a, b):
    return lax.dot_general(a, b, _NT, preferred_element_type=F32)


def _ffn_kernel(x_ref, npre_ref, wgu_ref, wd_ref, npost_ref, o_ref, h_ref, acc_ref, *, sub, fc):
    d_ff = wd_ref.shape[0]
    for r in range(x_ref.shape[0] // sub):
        rows = slice(r * sub, (r + 1) * sub)
        h_ref[rows, :] = (_rms(x_ref[rows, :]) * npre_ref[...]).astype(BF16)
        for j in range(d_ff // fc):
            h = h_ref[rows, :]
            g = _dot(h, wgu_ref[:, j * fc:(j + 1) * fc])
            u = _dot(h, wgu_ref[:, d_ff + j * fc:d_ff + (j + 1) * fc])
            a = (g * jax.nn.sigmoid(g) * u).astype(BF16)
            d = _dot(a, wd_ref[j * fc:(j + 1) * fc, :])
            if j == 0:
                acc_ref[rows, :] = d
            else:
                acc_ref[rows, :] += d
        o_ref[rows, :] = x_ref[rows, :] + FFN_SCALE * (_rms(acc_ref[rows, :]) * npost_ref[...])


def _ffn(x, npre, wgu, wd, npost):
    n, d = x.shape
    tm = min(ROW_TILE, n)
    return pl.pallas_call(
        functools.partial(_ffn_kernel, sub=min(ROW_SUB, tm), fc=FF_CHUNK),
        grid=(n // tm,),
        in_specs=[
            pl.BlockSpec((tm, d), lambda i: (i, 0)),
            _resident(npre.shape),
            _resident(wgu.shape),
            _resident(wd.shape),
            _resident(npost.shape),
        ],
        out_specs=pl.BlockSpec((tm, d), lambda i: (i, 0)),
        out_shape=jax.ShapeDtypeStruct((n, d), F32),
        scratch_shapes=[pltpu.VMEM((tm, d), BF16), pltpu.VMEM((tm, d), F32)],
        compiler_params=_params(1),
        name="ffn",
    )(x, npre, wgu, wd, npost)


def _mixin_kernel(x_ref, nw_ref, w_ref, wg_ref, z_ref, g_ref, *, sub, cw):
    for r in range(x_ref.shape[0] // sub):
        rows = slice(r * sub, (r + 1) * sub)
        h = (_rms(x_ref[rows, :]) * nw_ref[...]).astype(BF16)
        for j in range(w_ref.shape[1] // cw):
            z_ref[rows, j * cw:(j + 1) * cw] = _dot(h, w_ref[:, j * cw:(j + 1) * cw]).astype(BF16)
        g_ref[rows, :] = _dot(h, wg_ref[...])


def _mixin(x, nw, w, wg):
    n, d = x.shape
    nz = w.shape[1]
    tm = min(ROW_SUB, n)
    return pl.pallas_call(
        functools.partial(_mixin_kernel, sub=min(ROW_SUB, tm), cw=PROJ_CHUNK),
        grid=(n // tm,),
        in_specs=[
            pl.BlockSpec((tm, d), lambda i: (i, 0)),
            _resident(nw.shape),
            _resident(w.shape),
            _resident(wg.shape),
        ],
        out_specs=[
            pl.BlockSpec((tm, nz), lambda i: (i, 0)),
            pl.BlockSpec((tm, LANES), lambda i: (i, 0)),
        ],
        out_shape=[
            jax.ShapeDtypeStruct((n, nz), BF16),
            jax.ShapeDtypeStruct((n, LANES), F32),
        ],
        compiler_params=_params(1),
        name="mixin",
    )(x, nw, w, wg)


ONES_ROWS = 16


def _attn_kernel(q_ref, k_ref, v_ref, lam_ref, nw_ref, o_ref,
                 qq_ref, vt_ref, bias_ref, s_ref, cmax_ref, p_ref, alpha_ref, m_ref, acc_ref,
                 *, tq, tk, lam_init, hpg):
    s_len = v_ref.shape[0]
    dv = v_ref.shape[1] // hpg
    nq = s_len // tq
    n_blocks = nq * (nq + 1) // 2
    lane = lax.broadcasted_iota(jnp.int32, (1, LANES), 1)
    first_map = lane < A_DHEAD
    q_scale = (A_DHEAD ** -0.5) * LOG2E
    heads = range(hpg)

    ones_rows = (lax.broadcasted_iota(jnp.int32, (ONES_ROWS, tk), 0) == 0).astype(BF16)

    def vt_body(j, c):
        for hd in heads:
            v = v_ref[pl.ds(j * tk, tk), hd * dv:(hd + 1) * dv]
            vt_ref[hd, j, 0:dv, :] = v.astype(F32).T.astype(BF16)
            vt_ref[hd, j, dv:dv + ONES_ROWS, :] = ones_rows
        return c

    lax.fori_loop(0, s_len // tk, vt_body, 0)

    def qq_body(j, c):
        for hd in heads:
            q = q_ref[pl.ds(j * tq, tq), hd * dv:(hd + 1) * dv]
            q = (q.astype(F32) * q_scale).astype(BF16)
            zero = jnp.zeros_like(q)
            qq_ref[hd, j, 0:tq, :] = jnp.where(first_map, q, zero)
            qq_ref[hd, j, tq:2 * tq, :] = jnp.where(first_map, zero, q)
        return c

    lax.fori_loop(0, nq, qq_body, 0)

    @pl.when((pl.program_id(0) == 0) & (pl.program_id(1) == 0))
    def _():
        key = lax.broadcasted_iota(jnp.int32, (tk, 2 * tq), 0)
        col = lax.broadcasted_iota(jnp.int32, (tk, 2 * tq), 1)
        qry = jnp.where(col >= tq, col - tq, col)
        bias_ref[...] = jnp.where(key <= qry, 0.0, -jnp.inf).astype(F32)

        s_ref[...] = jnp.zeros(s_ref.shape, F32)
        p_ref[...] = jnp.zeros(p_ref.shape, BF16)
        cmax_ref[...] = jnp.zeros(cmax_ref.shape, F32)
        alpha_ref[...] = jnp.zeros(alpha_ref.shape, F32)
        m_ref[...] = jnp.zeros(m_ref.shape, F32)
        acc_ref[...] = jnp.zeros(acc_ref.shape, F32)

    lp = lam_ref[...]
    lam = (jnp.exp(jnp.sum(lp[0:1] * lp[1:2], axis=-1, keepdims=True))
           - jnp.exp(jnp.sum(lp[2:3] * lp[3:4], axis=-1, keepdims=True)) + lam_init)

    def stages(carry, diagonal, slot):
        qa, ka, qb, kb, qc, kc = carry
        other = 1 - slot

        for hd in heads:
            pv = _dot(vt_ref[hd, kc], p_ref[hd, slot])
            acc_ref[hd, slot] = alpha_ref[hd, slot] * acc_ref[hd, other] + pv

            m_old = jnp.where(kb == 0, -jnp.inf, m_ref[hd])
            m_new = jnp.maximum(m_old, cmax_ref[hd, other])
            p_ref[hd, other] = jnp.exp2(s_ref[hd, other] - m_new).astype(BF16)
            alpha_ref[hd, other] = jnp.exp2(m_old - m_new)
            m_ref[hd] = m_new

            k = k_ref[pl.ds(ka * tk, tk), hd * dv:(hd + 1) * dv]
            st = _dot_nt(k, qq_ref[hd, qa])
            if diagonal:
                st = st + bias_ref[...]
            s_ref[hd, slot] = st
            cmax_ref[hd, slot] = jnp.max(st, axis=0, keepdims=True)

        within = ka < qa
        more = qa + 1 < nq
        qa_n = jnp.where(within, qa, jnp.where(more, qa + 1, qa))
        ka_n = jnp.where(within, ka + 1, jnp.where(more, 0, ka))
        return qa_n, ka_n, qa, ka, qb, kb

    def iteration(i, carry, slot):
        qa, ka, _, _, qc, kc = carry
        new_carry = lax.cond(ka == qa,
                             lambda: stages(carry, True, slot),
                             lambda: stages(carry, False, slot))

        @pl.when((kc == qc) & (i >= 2) & (i < n_blocks + 2))
        def _():
            for hd in heads:
                acc = acc_ref[hd, slot]
                l_fin = acc[dv:dv + 1, :]
                inv_l = 1.0 / l_fin
                o_t = acc[:dv, :tq] * inv_l[:, :tq] - lam * (acc[:dv, tq:] * inv_l[:, tq:])
                r = lax.rsqrt(jnp.mean(o_t * o_t, axis=0, keepdims=True) + EPS)
                o = (o_t * r).T * nw_ref[...] * (1.0 - lam_init)
                o_ref[pl.ds(qc * tq, tq), hd * dv:(hd + 1) * dv] = o.astype(o_ref.dtype)

        return new_carry

    def pair(j, carry):
        carry = iteration(2 * j, carry, 0)
        return iteration(2 * j + 1, carry, 1)

    zero_i = jnp.int32(0)
    lax.fori_loop(0, (n_blocks + 3) // 2, pair, (zero_i,) * 6)


def _attention(z3, lam_p, nw, lam_init):
    b, s, _ = z3.shape
    dq = 2 * A_DHEAD
    tq = min(ATT_TQ, s)
    tk = min(ATT_TK, s)
    assert tq == tk
    hpg = ATT_HEADS_PER_STEP
    ng = A_HEADS // hpg
    once = pl.Buffered(1)
    return pl.pallas_call(
        functools.partial(_attn_kernel, tq=tq, tk=tk, lam_init=lam_init, hpg=hpg),
        grid=(b, ng),
        in_specs=[
            pl.BlockSpec((None, s, hpg * dq), lambda i, h: (i, 0, h), pipeline_mode=once),
            pl.BlockSpec((None, s, hpg * dq), lambda i, h: (i, 0, ng + h), pipeline_mode=once),
            pl.BlockSpec((None, s, hpg * dq), lambda i, h: (i, 0, 2 * ng + h), pipeline_mode=once),
            _resident(lam_p.shape),
            _resident(nw.shape),
        ],
        out_specs=pl.BlockSpec((None, s, hpg * dq), lambda i, h: (i, 0, h)),
        out_shape=jax.ShapeDtypeStruct((b, s, A_HEADS * dq), BF16),
        scratch_shapes=[
            pltpu.VMEM((hpg, s // tq, 2 * tq, dq), BF16),
            pltpu.VMEM((hpg, s // tk, dq + ONES_ROWS, tk), BF16),
            pltpu.VMEM((tk, 2 * tq), F32),
            pltpu.VMEM((hpg, 2, tk, 2 * tq), F32),
            pltpu.VMEM((hpg, 2, 1, 2 * tq), F32),
            pltpu.VMEM((hpg, 2, tk, 2 * tq), BF16),
            pltpu.VMEM((hpg, 2, 1, 2 * tq), F32),
            pltpu.VMEM((hpg, 1, 2 * tq), F32),
            pltpu.VMEM((hpg, 2, dq + ONES_ROWS, 2 * tq), F32),
        ],
        compiler_params=_params(2),
        name="diff_attn",
    )(z3, z3, z3, lam_p, nw)


def _mlstm_kernel(qk_ref, v_ref, og_ref, gn_ref, gt_ref, cw_ref, cb_ref, bn_ref, bt_ref, nw_ref,
                  o_ref, xbuf_ref, c_ref, m_ref, *, lc):
    nqk = M_HEADS * M_DQK
    c_idx = pl.program_id(1)

    @pl.when(c_idx == 0)
    def _():
        xbuf_ref[0:CONV_HALO, :] = jnp.zeros((CONV_HALO, xbuf_ref.shape[1]), F32)
        c_ref[...] = jnp.zeros(c_ref.shape, F32)
        m_ref[...] = jnp.zeros(m_ref.shape, F32)

    xbuf_ref[CONV_HALO:, :] = qk_ref[...].astype(F32)
    y = cb_ref[...]
    for i in range(CONV_K):
        off = CONV_HALO - (CONV_K - 1) + i
        y = y + cw_ref[i:i + 1, :] * xbuf_ref[off:off + lc, :]
    xbuf_ref[0:CONV_HALO, :] = xbuf_ref[lc:lc + CONV_HALO, :]
    a = y * jax.nn.sigmoid(y)
    q_all = a[:, :nqk] * (M_DQK ** -0.5)
    k_all = a[:, nqk:]

    gn = gn_ref[...] + bn_ref[...]
    logf_n = jax.nn.log_sigmoid(gn) * LOG2E
    r_i = lax.broadcasted_iota(jnp.int32, (lc, lc), 0)
    c_i = lax.broadcasted_iota(jnp.int32, (lc, lc), 1)
    causal = c_i <= r_i
    tri = causal.astype(F32)
    bcum_n = jnp.dot(tri, logf_n, precision=lax.Precision.HIGHEST, preferred_element_type=F32)
    gt = gt_ref[...] + bt_ref[...]
    ig_t = gt[0:M_HEADS] * LOG2E
    logf_t = jax.nn.log_sigmoid(gt[M_HEADS:2 * M_HEADS]) * LOG2E
    bcum_t = lax.dot_general(logf_t, tri, _NT, precision=lax.Precision.HIGHEST,
                             preferred_element_type=F32)
    b_last = bcum_t[:, lc - 1:lc]

    lane = lax.broadcasted_iota(jnp.int32, (1, LANES), 1)
    sub = lax.broadcasted_iota(jnp.int32, (LANES, 1), 0)
    ones_blk = jnp.ones((lc, LANES), BF16)
    rep = lc // LANES

    for pj in range(M_HEADS // 2):
        qp = q_all[:, pj * LANES:(pj + 1) * LANES]
        kp = k_all[:, pj * LANES:(pj + 1) * LANES]
        kb = kp.astype(BF16)
        kpt = kp.T
        c_old = c_ref[pj]
        c_bf = c_old.astype(BF16)
        row_scale = jnp.zeros((LANES, 1), F32)
        upd = jnp.zeros(c_old.shape, F32)
        for half in range(2):
            h = 2 * pj + half
            in_head_l = (lane >= half * M_DQK) & (lane < (half + 1) * M_DQK)
            in_head_s = (sub >= half * M_DQK) & (sub < (half + 1) * M_DQK)
            qh = jnp.where(in_head_l, qp, 0.0).astype(BF16)
            vext = jnp.concatenate([v_ref[:, h * M_DV:(h + 1) * M_DV], ones_blk], axis=1)

            s = _dot_nt(qh, kb)
            bc = jnp.broadcast_to(bcum_n[:, M_HEADS + h:M_HEADS + h + 1], (lc, LANES))
            br = bcum_t[h:h + 1, :]
            ir = ig_t[h:h + 1, :]
            dm = jnp.where(causal, jnp.concatenate([bc] * rep, axis=1) - br + ir, -jnp.inf)
            m_prev = m_ref[h:h + 1, 0:1]
            m_inter = bc + m_prev
            m_t = jnp.maximum(m_inter, jnp.max(dm, axis=-1, keepdims=True))
            p = (jnp.exp2(dm - jnp.concatenate([m_t] * rep, axis=1)) * s).astype(BF16)
            sc = jnp.exp2(m_inter - m_t)
            tot = jnp.concatenate([sc, sc], axis=1) * _dot(qh, c_bf) + _dot(p, vext)
            num = tot[:, :M_DV]
            den = tot[:, M_DV:]
            hh = num / jnp.maximum(jnp.abs(den), jnp.exp2(-m_t))
            hn = _rms(hh) * nw_ref[h:h + 1, :]
            og = og_ref[:, h * M_DV:(h + 1) * M_DV].astype(F32)
            o_ref[:, h * M_DV:(h + 1) * M_DV] = (jax.nn.sigmoid(og) * hn).astype(o_ref.dtype)

            bl = b_last[h:h + 1, :]
            a_row = bl - br + ir
            m_new = jnp.maximum(bl + m_prev, jnp.max(a_row, axis=-1, keepdims=True))
            w_row = jnp.exp2(a_row - m_new)
            sp = jnp.exp2(bl + m_prev - m_new)
            kw = jnp.where(in_head_s, kpt * w_row, 0.0).astype(BF16)
            upd = upd + _dot(kw, vext)
            row_scale = jnp.where(in_head_s, sp, row_scale)
            m_ref[h:h + 1, :] = jnp.broadcast_to(m_new, (1, LANES))
        c_ref[pj] = row_scale * c_old + upd


def _mlstm(z3, gates_n, gates_t, cw, cb, bias_n, bias_t, nw):
    b, s, _ = z3.shape
    lc = min(M_CHUNK, s)
    nqk2 = 2 * M_HEADS * M_DQK
    nv = M_HEADS * M_DV
    qk_blk = (3 * A_HEADS * 2 * A_DHEAD) // nqk2
    v_blk = qk_blk + 1
    og_blk = qk_blk + 2
    return pl.pallas_call(
        functools.partial(_mlstm_kernel, lc=lc),
        grid=(b, s // lc),
        in_specs=[
            pl.BlockSpec((None, lc, nqk2), lambda i, c: (i, c, qk_blk)),
            pl.BlockSpec((None, lc, nv), lambda i, c: (i, c, v_blk)),
            pl.BlockSpec((None, lc, nv), lambda i, c: (i, c, og_blk)),
            pl.BlockSpec((None, lc, LANES), lambda i, c: (i, c, 0)),
            pl.BlockSpec((None, 2 * M_HEADS, lc), lambda i, c: (i, 0, c)),
            _resident(cw.shape),
            _resident(cb.shape),
            _resident(bias_n.shape),
            _resident(bias_t.shape),
            _resident(nw.shape),
        ],
        out_specs=pl.BlockSpec((None, lc, nv), lambda i, c: (i, c, 0)),
        out_shape=jax.ShapeDtypeStruct((b, s, nv), BF16),
        scratch_shapes=[
            pltpu.VMEM((lc + CONV_HALO, nqk2), F32),
            pltpu.VMEM((M_HEADS // 2, LANES, M_DV + LANES), F32),
            pltpu.VMEM((M_HEADS, LANES), F32),
        ],
        compiler_params=_params(2),
        name="mlstm",
    )(z3, z3, z3, gates_n, gates_t, cw, cb, bias_n, bias_t, nw)


def _merge_kernel(x_ref, ya_ref, ym_ref, gl_ref, gb_ref, wa_ref, wm_ref, wo_ref, nw_ref, o_ref):
    d = x_ref.shape[1]
    g = jax.nn.sigmoid(gl_ref[...].astype(F32) + gb_ref[...])
    merged = g[:, :d] * _dot(ya_ref[...], wa_ref[...]) + g[:, d:] * _dot(ym_ref[...], wm_ref[...])
    y = _dot(merged.astype(BF16), wo_ref[...])
    o_ref[...] = x_ref[...] + _rms(y) * nw_ref[...]


def _merge(x, ya, ym, z, gb, wa, wm, wo, nw):
    n, d = x.shape
    tm = min(ROW_SUB, n)
    gl_blk = (z.shape[1] - 2 * d) // (2 * d)
    return pl.pallas_call(
        _merge_kernel,
        grid=(n // tm,),
        in_specs=[
            pl.BlockSpec((tm, d), lambda i: (i, 0)),
            pl.BlockSpec((tm, d), lambda i: (i, 0)),
            pl.BlockSpec((tm, d), lambda i: (i, 0)),
            pl.BlockSpec((tm, 2 * d), lambda i: (i, gl_blk)),
            _resident(gb.shape),
            _resident(wa.shape),
            _resident(wm.shape),
            _resident(wo.shape),
            _resident(nw.shape),
        ],
        out_specs=pl.BlockSpec((tm, d), lambda i: (i, 0)),
        out_shape=jax.ShapeDtypeStruct((n, d), F32),
        compiler_params=_params(1),
        name="merge",
    )(x, ya, ym, z, gb, wa, wm, wo, nw)


def kernel(x, ffn1_norm_pre, ffn1_w_gu, ffn1_w_down, ffn1_norm_post, mix_norm_pre, w_in,
           attn_lam_q1, attn_lam_k1, attn_lam_q2, attn_lam_k2, attn_norm_w, conv_w, conv_b,
           igate_b, fgate_b, mlstm_norm_w, w_proj_a, w_proj_m, gate_b, w_out, mix_norm_post,
           ffn2_norm_pre, ffn2_w_gu, ffn2_w_down, ffn2_norm_post):
    b, s, d = x.shape
    n = b * s
    a_w = 3 * A_HEADS * 2 * A_DHEAD
    m_w = 2 * M_HEADS * M_DQK + 2 * M_HEADS * M_DV
    g0 = a_w + m_w
    g1 = g0 + 2 * M_HEADS
    row = lambda v: v.reshape(1, -1)

    xf = x.reshape(n, d)
    for l in range(DEPTH):
        wgu, wd = ffn1_w_gu[l].astype(BF16), ffn1_w_down[l].astype(BF16)
        xf = _ffn(xf, row(ffn1_norm_pre[l]), wgu, wd, row(ffn1_norm_post[l]))

        w_main = jnp.concatenate([w_in[l][:, :g0], w_in[l][:, g1:]], axis=1).astype(BF16)
        w_gate = jnp.pad(w_in[l][:, g0:g1], ((0, 0), (0, LANES - 2 * M_HEADS))).astype(BF16)
        z, gates = _mixin(xf, row(mix_norm_pre[l]), w_main, w_gate)
        z3 = z.reshape(b, s, -1)

        lam_init = 0.8 - 0.6 * math.exp(-0.3 * l)
        lam_p = jnp.stack([attn_lam_q1[l], attn_lam_k1[l], attn_lam_q2[l], attn_lam_k2[l]])
        ya = _attention(z3, lam_p, row(attn_norm_w[l]), lam_init)

        gates_n = gates.reshape(b, s, LANES)
        gates_t = gates_n[:, :, :2 * M_HEADS].transpose(0, 2, 1)
        gate_bias = jnp.concatenate([igate_b[l], fgate_b[l]])
        bias_n = jnp.pad(gate_bias, (0, LANES - 2 * M_HEADS)).reshape(1, LANES)
        bias_t = gate_bias.reshape(2 * M_HEADS, 1)
        ym = _mlstm(z3, gates_n, gates_t, conv_w[l], row(conv_b[l]), bias_n, bias_t, mlstm_norm_w[l])

        xf = _merge(xf, ya.reshape(n, -1), ym.reshape(n, -1), z, row(gate_b[l]),
                    w_proj_a[l].astype(BF16), w_proj_m[l].astype(BF16), w_out[l].astype(BF16),
                    row(mix_norm_post[l]))

        wgu, wd = ffn2_w_gu[l].astype(BF16), ffn2_w_down[l].astype(BF16)
        xf = _ffn(xf, row(ffn2_norm_pre[l]), wgu, wd, row(ffn2_norm_post[l]))
    return xf.reshape(b, s, d)
```

```python
import functools
import math

import jax
import jax.numpy as jnp
from jax import lax
from jax.experimental import pallas as pl
from jax.experimental.pallas import tpu as pltpu

F32 = jnp.float32
BF16 = jnp.bfloat16

EPS = 1e-6
DEPTH = 2
A_HEADS = 8
A_DHEAD = 64
M_HEADS = 8
M_DQK = 64
M_DV = 128
CONV_K = 4
D_FF = 2816
FFN_SCALE = 0.5

LANES = 128
VMEM_LIMIT_BYTES = 56 * 1024 * 1024
LOG2E = math.log2(math.e)

ROW_TILE = 1024
ROW_SUB = 512
FF_CHUNK = 256
PROJ_CHUNK = 512
ATT_TQ = 512
ATT_TK = 512
ATT_HEADS_PER_STEP = 2
M_CHUNK = 256
CONV_HALO = 8

_NT = (((1,), (1,)), ((), ()))


def _params(n_axes, flags=None):
    return pltpu.CompilerParams(dimension_semantics=("arbitrary",) * n_axes,
                                vmem_limit_bytes=VMEM_LIMIT_BYTES, flags=flags)


def _resident(shape):
    nd = len(shape)
    return pl.BlockSpec(shape, lambda *_: (0,) * nd, pipeline_mode=pl.Buffered(1))


def _rms(x):
    return x * lax.rsqrt(jnp.mean(x * x, axis=-1, keepdims=True) + EPS)


def _dot(a, b):
    return jnp.dot(a, b, preferred_element_type=F32)


def _dot_nt(a, b):
    return lax.dot_general(a, b, _NT, preferred_element_type=F32)


def _ffn_kernel(x_ref, npre_ref, wgu_ref, wd_ref, npost_ref, o_ref, h_ref, acc_ref, *, sub, fc):
    d_ff = wd_ref.shape[0]
    for r in range(x_ref.shape[0] // sub):
        rows = slice(r * sub, (r + 1) * sub)
        h_ref[rows, :] = (_rms(x_ref[rows, :]) * npre_ref[...]).astype(BF16)
        for j in range(d_ff // fc):
            h = h_ref[rows, :]
            g = _dot(h, wgu_ref[:, j * fc:(j + 1) * fc])
            u = _dot(h, wgu_ref[:, d_ff + j * fc:d_ff + (j + 1) * fc])
            a = (g * jax.nn.sigmoid(g) * u).astype(BF16)
            d = _dot(a, wd_ref[j * fc:(j + 1) * fc, :])
            if j == 0:
                acc_ref[rows, :] = d
            else:
                acc_ref[rows, :] += d
        o_ref[rows, :] = x_ref[rows, :] + FFN_SCALE * (_rms(acc_ref[rows, :]) * npost_ref[...])


def _ffn(x, npre, wgu, wd, npost):
    n, d = x.shape
    tm = min(ROW_TILE, n)
    return pl.pallas_call(
        functools.partial(_ffn_kernel, sub=min(ROW_SUB, tm), fc=FF_CHUNK),
        grid=(n // tm,),
        in_specs=[
            pl.BlockSpec((tm, d), lambda i: (i, 0)),
            _resident(npre.shape),
            _resident(wgu.shape),
            _resident(wd.shape),
            _resident(npost.shape),
        ],
        out_specs=pl.BlockSpec((tm, d), lambda i: (i, 0)),
        out_shape=jax.ShapeDtypeStruct((n, d), F32),
        scratch_shapes=[pltpu.VMEM((tm, d), BF16), pltpu.VMEM((tm, d), F32)],
        compiler_params=_params(1),
        name="ffn",
    )(x, npre, wgu, wd, npost)


def _mixin_kernel(x_ref, nw_ref, w_ref, wg_ref, z_ref, g_ref, *, sub, cw):
    for r in range(x_ref.shape[0] // sub):
        rows = slice(r * sub, (r + 1) * sub)
        h = (_rms(x_ref[rows, :]) * nw_ref[...]).astype(BF16)
        for j in range(w_ref.shape[1] // cw):
            z_ref[rows, j * cw:(j + 1) * cw] = _dot(h, w_ref[:, j * cw:(j + 1) * cw]).astype(BF16)
        g_ref[rows, :] = _dot(h, wg_ref[...])


def _mixin(x, nw, w, wg):
    n, d = x.shape
    nz = w.shape[1]
    tm = min(ROW_SUB, n)
    return pl.pallas_call(
        functools.partial(_mixin_kernel, sub=min(ROW_SUB, tm), cw=PROJ_CHUNK),
        grid=(n // tm,),
        in_specs=[
            pl.BlockSpec((tm, d), lambda i: (i, 0)),
            _resident(nw.shape),
            _resident(w.shape),
            _resident(wg.shape),
        ],
        out_specs=[
            pl.BlockSpec((tm, nz), lambda i: (i, 0)),
            pl.BlockSpec((tm, LANES), lambda i: (i, 0)),
        ],
        out_shape=[
            jax.ShapeDtypeStruct((n, nz), BF16),
            jax.ShapeDtypeStruct((n, LANES), F32),
        ],
        compiler_params=_params(1),
        name="mixin",
    )(x, nw, w, wg)


ONES_ROWS = 16


def _attn_kernel(q_ref, k_ref, v_ref, lam_ref, nw_ref, o_ref,
                 qq_ref, vt_ref, bias_ref, s_ref, cmax_ref, p_ref, alpha_ref, m_ref, acc_ref,
                 *, tq, tk, lam_init, hpg):
    s_len = v_ref.shape[0]
    dv = v_ref.shape[1] // hpg
    nq = s_len // tq
    n_blocks = nq * (nq + 1) // 2
    lane = lax.broadcasted_iota(jnp.int32, (1, LANES), 1)
    first_map = lane < A_DHEAD
    q_scale = (A_DHEAD ** -0.5) * LOG2E
    heads = range(hpg)

    ones_rows = (lax.broadcasted_iota(jnp.int32, (ONES_ROWS, tk), 0) == 0).astype(BF16)

    def vt_body(j, c):
        for hd in heads:
            v = v_ref[pl.ds(j * tk, tk), hd * dv:(hd + 1) * dv]
            vt_ref[hd, j, 0:dv, :] = v.astype(F32).T.astype(BF16)
            vt_ref[hd, j, dv:dv + ONES_ROWS, :] = ones_rows
        return c

    lax.fori_loop(0, s_len // tk, vt_body, 0)

    def qq_body(j, c):
        for hd in heads:
            q = q_ref[pl.ds(j * tq, tq), hd * dv:(hd + 1) * dv]
            q = (q.astype(F32) * q_scale).astype(BF16)
            zero = jnp.zeros_like(q)
            qq_ref[hd, j, 0:tq, :] = jnp.where(first_map, q, zero)
            qq_ref[hd, j, tq:2 * tq, :] = jnp.where(first_map, zero, q)
        return c

    lax.fori_loop(0, nq, qq_body, 0)

    @pl.when((pl.program_id(0) == 0) & (pl.program_id(1) == 0))
    def _():
        key = lax.broadcasted_iota(jnp.int32, (tk, 2 * tq), 0)
        col = lax.broadcasted_iota(jnp.int32, (tk, 2 * tq), 1)
        qry = jnp.where(col >= tq, col - tq, col)
        bias_ref[...] = jnp.where(key <= qry, 0.0, -jnp.inf).astype(F32)

        s_ref[...] = jnp.zeros(s_ref.shape, F32)
        p_ref[...] = jnp.zeros(p_ref.shape, BF16)
        cmax_ref[...] = jnp.zeros(cmax_ref.shape, F32)
        alpha_ref[...] = jnp.zeros(alpha_ref.shape, F32)
        m_ref[...] = jnp.zeros(m_ref.shape, F32)
        acc_ref[...] = jnp.zeros(acc_ref.shape, F32)

    lp = lam_ref[...]
    lam = (jnp.exp(jnp.sum(lp[0:1] * lp[1:2], axis=-1, keepdims=True))
           - jnp.exp(jnp.sum(lp[2:3] * lp[3:4], axis=-1, keepdims=True)) + lam_init)

    def stages(carry, diagonal, slot):
        qa, ka, qb, kb, qc, kc = carry
        other = 1 - slot

        for hd in heads:
            pv = _dot(vt_ref[hd, kc], p_ref[hd, slot])
            acc_ref[hd, slot] = alpha_ref[hd, slot] * acc_ref[hd, other] + pv

            m_old = jnp.where(kb == 0, -jnp.inf, m_ref[hd])
            m_new = jnp.maximum(m_old, cmax_ref[hd, other])
            p_ref[hd, other] = jnp.exp2(s_ref[hd, other] - m_new).astype(BF16)
            alpha_ref[hd, other] = jnp.exp2(m_old - m_new)
            m_ref[hd] = m_new

            k = k_ref[pl.ds(ka * tk, tk), hd * dv:(hd + 1) * dv]
            st = _dot_nt(k, qq_ref[hd, qa])
            if diagonal:
                st = st + bias_ref[...]
            s_ref[hd, slot] = st
            cmax_ref[hd, slot] = jnp.max(st, axis=0, keepdims=True)

        within = ka < qa
        more = qa + 1 < nq
        qa_n = jnp.where(within, qa, jnp.where(more, qa + 1, qa))
        ka_n = jnp.where(within, ka + 1, jnp.where(more, 0, ka))
        return qa_n, ka_n, qa, ka, qb, kb

    def iteration(i, carry, slot):
        qa, ka, _, _, qc, kc = carry
        new_carry = lax.cond(ka == qa,
                             lambda: stages(carry, True, slot),
                             lambda: stages(carry, False, slot))

        @pl.when((kc == qc) & (i >= 2) & (i < n_blocks + 2))
        def _():
            for hd in heads:
                acc = acc_ref[hd, slot]
                l_fin = acc[dv:dv + 1, :]
                inv_l = 1.0 / l_fin
                o_t = acc[:dv, :tq] * inv_l[:, :tq] - lam * (acc[:dv, tq:] * inv_l[:, tq:])
                r = lax.rsqrt(jnp.mean(o_t * o_t, axis=0, keepdims=True) + EPS)
                o = (o_t * r).T * nw_ref[...] * (1.0 - lam_init)
                o_ref[pl.ds(qc * tq, tq), hd * dv:(hd + 1) * dv] = o.astype(o_ref.dtype)

        return new_carry

    def pair(j, carry):
        carry = iteration(2 * j, carry, 0)
        return iteration(2 * j + 1, carry, 1)

    zero_i = jnp.int32(0)
    lax.fori_loop(0, (n_blocks + 3) // 2, pair, (zero_i,) * 6)


def _attention(z3, lam_p, nw, lam_init):
    b, s, _ = z3.shape
    dq = 2 * A_DHEAD
    tq = min(ATT_TQ, s)
    tk = min(ATT_TK, s)
    assert tq == tk
    hpg = ATT_HEADS_PER_STEP
    ng = A_HEADS // hpg
    return pl.pallas_call(
        functools.partial(_attn_kernel, tq=tq, tk=tk, lam_init=lam_init, hpg=hpg),
        grid=(b, ng),
        in_specs=[
            pl.BlockSpec((None, s, hpg * dq), lambda i, h: (i, 0, h)),
            pl.BlockSpec((None, s, hpg * dq), lambda i, h: (i, 0, ng + h)),
            pl.BlockSpec((None, s, hpg * dq), lambda i, h: (i, 0, 2 * ng + h)),
            _resident(lam_p.shape),
            _resident(nw.shape),
        ],
        out_specs=pl.BlockSpec((None, s, hpg * dq), lambda i, h: (i, 0, h)),
        out_shape=jax.ShapeDtypeStruct((b, s, A_HEADS * dq), BF16),
        scratch_shapes=[
            pltpu.VMEM((hpg, s // tq, 2 * tq, dq), BF16),
            pltpu.VMEM((hpg, s // tk, dq + ONES_ROWS, tk), BF16),
            pltpu.VMEM((tk, 2 * tq), F32),
            pltpu.VMEM((hpg, 2, tk, 2 * tq), F32),
            pltpu.VMEM((hpg, 2, 1, 2 * tq), F32),
            pltpu.VMEM((hpg, 2, tk, 2 * tq), BF16),
            pltpu.VMEM((hpg, 2, 1, 2 * tq), F32),
            pltpu.VMEM((hpg, 1, 2 * tq), F32),
            pltpu.VMEM((hpg, 2, dq + ONES_ROWS, 2 * tq), F32),
        ],
        compiler_params=_params(2),
        name="diff_attn",
    )(z3, z3, z3, lam_p, nw)


def _mlstm_kernel(qk_ref, v_ref, og_ref, gn_ref, gt_ref, cw_ref, cb_ref, bn_ref, bt_ref, nw_ref,
                  o_ref, xbuf_ref, c_ref, m_ref, *, lc):
    nqk = M_HEADS * M_DQK
    c_idx = pl.program_id(1)

    @pl.when(c_idx == 0)
    def _():
        xbuf_ref[0:CONV_HALO, :] = jnp.zeros((CONV_HALO, xbuf_ref.shape[1]), F32)
        c_ref[...] = jnp.zeros(c_ref.shape, F32)
        m_ref[...] = jnp.zeros(m_ref.shape, F32)

    xbuf_ref[CONV_HALO:, :] = qk_ref[...].astype(F32)
    y = cb_ref[...]
    for i in range(CONV_K):
        off = CONV_HALO - (CONV_K - 1) + i
        y = y + cw_ref[i:i + 1, :] * xbuf_ref[off:off + lc, :]
    xbuf_ref[0:CONV_HALO, :] = xbuf_ref[lc:lc + CONV_HALO, :]
    a = y * jax.nn.sigmoid(y)
    q_all = a[:, :nqk] * (M_DQK ** -0.5)
    k_all = a[:, nqk:]

    gn = gn_ref[...] + bn_ref[...]
    logf_n = jax.nn.log_sigmoid(gn) * LOG2E
    r_i = lax.broadcasted_iota(jnp.int32, (lc, lc), 0)
    c_i = lax.broadcasted_iota(jnp.int32, (lc, lc), 1)
    causal = c_i <= r_i
    tri = causal.astype(F32)
    bcum_n = jnp.dot(tri, logf_n, precision=lax.Precision.HIGHEST, preferred_element_type=F32)
    gt = gt_ref[...] + bt_ref[...]
    ig_t = gt[0:M_HEADS] * LOG2E
    logf_t = jax.nn.log_sigmoid(gt[M_HEADS:2 * M_HEADS]) * LOG2E
    bcum_t = lax.dot_general(logf_t, tri, _NT, precision=lax.Precision.HIGHEST,
                             preferred_element_type=F32)
    b_last = bcum_t[:, lc - 1:lc]

    lane = lax.broadcasted_iota(jnp.int32, (1, LANES), 1)
    sub = lax.broadcasted_iota(jnp.int32, (LANES, 1), 0)
    ones_blk = jnp.ones((lc, LANES), BF16)
    rep = lc // LANES

    for pj in range(M_HEADS // 2):
        qp = q_all[:, pj * LANES:(pj + 1) * LANES]
        kp = k_all[:, pj * LANES:(pj + 1) * LANES]
        kb = kp.astype(BF16)
        kpt = kp.T
        c_old = c_ref[pj]
        c_bf = c_old.astype(BF16)
        row_scale = jnp.zeros((LANES, 1), F32)
        upd = jnp.zeros(c_old.shape, F32)
        for half in range(2):
            h = 2 * pj + half
            in_head_l = (lane >= half * M_DQK) & (lane < (half + 1) * M_DQK)
            in_head_s = (sub >= half * M_DQK) & (sub < (half + 1) * M_DQK)
            qh = jnp.where(in_head_l, qp, 0.0).astype(BF16)
            vext = jnp.concatenate([v_ref[:, h * M_DV:(h + 1) * M_DV], ones_blk], axis=1)

            s = _dot_nt(qh, kb)
            bc = jnp.broadcast_to(bcum_n[:, M_HEADS + h:M_HEADS + h + 1], (lc, LANES))
            br = bcum_t[h:h + 1, :]
            ir = ig_t[h:h + 1, :]
            dm = jnp.where(causal, jnp.concatenate([bc] * rep, axis=1) - br + ir, -jnp.inf)
            m_prev = m_ref[h:h + 1, 0:1]
            m_inter = bc + m_prev
            m_t = jnp.maximum(m_inter, jnp.max(dm, axis=-1, keepdims=True))
            p = (jnp.exp2(dm - jnp.concatenate([m_t] * rep, axis=1)) * s).astype(BF16)
            sc = jnp.exp2(m_inter - m_t)
            tot = jnp.concatenate([sc, sc], axis=1) * _dot(qh, c_bf) + _dot(p, vext)
            num = tot[:, :M_DV]
            den = tot[:, M_DV:]
            hh = num / jnp.maximum(jnp.abs(den), jnp.exp2(-m_t))
            hn = _rms(hh) * nw_ref[h:h + 1, :]
            og = og_ref[:, h * M_DV:(h + 1) * M_DV].astype(F32)
            o_ref[:, h * M_DV:(h + 1) * M_DV] = (jax.nn.sigmoid(og) * hn).astype(o_ref.dtype)

            bl = b_last[h:h + 1, :]
            a_row = bl - br + ir
            m_new = jnp.maximum(bl + m_prev, jnp.max(a_row, axis=-1, keepdims=True))
            w_row = jnp.exp2(a_row - m_new)
            sp = jnp.exp2(bl + m_prev - m_new)
            kw = jnp.where(in_head_s, kpt * w_row, 0.0).astype(BF16)
            upd = upd + _dot(kw, vext)
            row_scale = jnp.where(in_head_s, sp, row_scale)
            m_ref[h:h + 1, :] = jnp.broadcast_to(m_new, (1, LANES))
        c_ref[pj] = row_scale * c_old + upd


def _mlstm(z3, gates_n, gates_t, cw, cb, bias_n, bias_t, nw):
    b, s, _ = z3.shape
    lc = min(M_CHUNK, s)
    nqk2 = 2 * M_HEADS * M_DQK
    nv = M_HEADS * M_DV
    qk_blk = (3 * A_HEADS * 2 * A_DHEAD) // nqk2
    v_blk = qk_blk + 1
    og_blk = qk_blk + 2
    return pl.pallas_call(
        functools.partial(_mlstm_kernel, lc=lc),
        grid=(b, s // lc),
        in_specs=[
            pl.BlockSpec((None, lc, nqk2), lambda i, c: (i, c, qk_blk)),
            pl.BlockSpec((None, lc, nv), lambda i, c: (i, c, v_blk)),
            pl.BlockSpec((None, lc, nv), lambda i, c: (i, c, og_blk)),
            pl.BlockSpec((None, lc, LANES), lambda i, c: (i, c, 0)),
            pl.BlockSpec((None, 2 * M_HEADS, lc), lambda i, c: (i, 0, c)),
            _resident(cw.shape),
            _resident(cb.shape),
            _resident(bias_n.shape),
            _resident(bias_t.shape),
            _resident(nw.shape),
        ],
        out_specs=pl.BlockSpec((None, lc, nv), lambda i, c: (i, c, 0)),
        out_shape=jax.ShapeDtypeStruct((b, s, nv), BF16),
        scratch_shapes=[
            pltpu.VMEM((lc + CONV_HALO, nqk2), F32),
            pltpu.VMEM((M_HEADS // 2, LANES, M_DV + LANES), F32),
            pltpu.VMEM((M_HEADS, LANES), F32),
        ],
        compiler_params=_params(2),
        name="mlstm",
    )(z3, z3, z3, gates_n, gates_t, cw, cb, bias_n, bias_t, nw)


def _merge_kernel(x_ref, ya_ref, ym_ref, gl_ref, gb_ref, wa_ref, wm_ref, wo_ref, nw_ref, o_ref):
    d = x_ref.shape[1]
    g = jax.nn.sigmoid(gl_ref[...].astype(F32) + gb_ref[...])
    merged = g[:, :d] * _dot(ya_ref[...], wa_ref[...]) + g[:, d:] * _dot(ym_ref[...], wm_ref[...])
    y = _dot(merged.astype(BF16), wo_ref[...])
    o_ref[...] = x_ref[...] + _rms(y) * nw_ref[...]


def _merge(x, ya, ym, z, gb, wa, wm, wo, nw):
    n, d = x.shape
    tm = min(ROW_SUB, n)
    gl_blk = (z.shape[1] - 2 * d) // (2 * d)
    return pl.pallas_call(
        _merge_kernel,
        grid=(n // tm,),
        in_specs=[
            pl.BlockSpec((tm, d), lambda i: (i, 0)),
            pl.BlockSpec((tm, d), lambda i: (i, 0)),
            pl.BlockSpec((tm, d), lambda i: (i, 0)),
            pl.BlockSpec((tm, 2 * d), lambda i: (i, gl_blk)),
            _resident(gb.shape),
            _resident(wa.shape),
            _resident(wm.shape),
            _resident(wo.shape),
            _resident(nw.shape),
        ],
        out_specs=pl.BlockSpec((tm, d), lambda i: (i, 0)),
        out_shape=jax.ShapeDtypeStruct((n, d), F32),
        compiler_params=_params(1),
        name="merge",
    )(x, ya, ym, z, gb, wa, wm, wo, nw)


def kernel(x, ffn1_norm_pre, ffn1_w_gu, ffn1_w_down, ffn1_norm_post, mix_norm_pre, w_in,
           attn_lam_q1, attn_lam_k1, attn_lam_q2, attn_lam_k2, attn_norm_w, conv_w, conv_b,
           igate_b, fgate_b, mlstm_norm_w, w_proj_a, w_proj_m, gate_b, w_out, mix_norm_post,
           ffn2_norm_pre, ffn2_w_gu, ffn2_w_down, ffn2_norm_post):
    b, s, d = x.shape
    n = b * s
    a_w = 3 * A_HEADS * 2 * A_DHEAD
    m_w = 2 * M_HEADS * M_DQK + 2 * M_HEADS * M_DV
    g0 = a_w + m_w
    g1 = g0 + 2 * M_HEADS
    row = lambda v: v.reshape(1, -1)

    xf = x.reshape(n, d)
    for l in range(DEPTH):
        wgu, wd = ffn1_w_gu[l].astype(BF16), ffn1_w_down[l].astype(BF16)
        xf = _ffn(xf, row(ffn1_norm_pre[l]), wgu, wd, row(ffn1_norm_post[l]))

        w_main = jnp.concatenate([w_in[l][:, :g0], w_in[l][:, g1:]], axis=1).astype(BF16)
        w_gate = jnp.pad(w_in[l][:, g0:g1], ((0, 0), (0, LANES - 2 * M_HEADS))).astype(BF16)
        z, gates = _mixin(xf, row(mix_norm_pre[l]), w_main, w_gate)
        z3 = z.reshape(b, s, -1)

        lam_init = 0.8 - 0.6 * math.exp(-0.3 * l)
        lam_p = jnp.stack([attn_lam_q1[l], attn_lam_k1[l], attn_lam_q2[l], attn_lam_k2[l]])
        ya = _attention(z3, lam_p, row(attn_norm_w[l]), lam_init)

        gates_n = gates.reshape(b, s, LANES)
        gates_t = gates_n[:, :, :2 * M_HEADS].transpose(0, 2, 1)
        gate_bias = jnp.concatenate([igate_b[l], fgate_b[l]])
        bias_n = jnp.pad(gate_bias, (0, LANES - 2 * M_HEADS)).reshape(1, LANES)
        bias_t = gate_bias.reshape(2 * M_HEADS, 1)
        ym = _mlstm(z3, gates_n, gates_t, conv_w[l], row(conv_b[l]), bias_n, bias_t, mlstm_norm_w[l])

        xf = _merge(xf, ya.reshape(n, -1), ym.reshape(n, -1), z, row(gate_b[l]),
                    w_proj_a[l].astype(BF16), w_proj_m[l].astype(BF16), w_out[l].astype(BF16),
                    row(mix_norm_post[l]))

        wgu, wd = ffn2_w_gu[l].astype(BF16), ffn2_w_down[l].astype(BF16)
        xf = _ffn(xf, row(ffn2_norm_pre[l]), wgu, wd, row(ffn2_norm_post[l]))
    return xf.reshape(b, s, d)
```
